```python
import math
import jax
import jax.numpy as jnp
from jax import lax
import numpy as np


D_MODEL = 1024
BATCH = 1
SEQ = 16384
DEPTH = 1

CHUNK = 64
EPS = 1e-6
A_HEADS = 8
A_HEAD_DIM = 128
A_WIDTH = A_HEADS * A_HEAD_DIM
B_HEADS = 16
B_QK_DIM = 64
B_V_DIM = 64
B_WIDTH = B_HEADS * B_V_DIM
Q_RANK = 256
KV_RANK = 128
IDX_HEADS = 8
IDX_DIM = 64
TOPK_MAX = 256
Q_BLOCK = 128
N_BUCKETS = 32
MAX_DISTANCE = 128
D_FF = 2816
CONV_WIDTH = 3
IN_SPLITS = (A_WIDTH, A_WIDTH, A_WIDTH, A_WIDTH, Q_RANK, KV_RANK, IDX_DIM, IDX_HEADS, D_MODEL, D_MODEL)
N_IN = sum(IN_SPLITS)

kernel_name = 'hybrid_hgrn2_dsa_convffn_block'


def rms_norm(x, w):
    xf = x.astype(jnp.float32)
    y = xf * lax.rsqrt(jnp.mean(xf * xf, axis=-1, keepdims=True) + EPS)
    return (y * w.astype(jnp.float32)).astype(x.dtype)


def split_cols(h):
    offs = [int(o) for o in np.cumsum(IN_SPLITS)[:-1]]
    return jnp.split(h, offs, axis=-1)


def t5_bucket(rel):
    half = N_BUCKETS // 2
    max_exact = half // 2
    ret = jnp.where(rel > 0, half, 0)
    n = jnp.abs(rel)
    nf = jnp.maximum(n, 1).astype(jnp.float32)
    large = max_exact + (jnp.log(nf / max_exact) / math.log(MAX_DISTANCE / max_exact) * (half - max_exact)).astype(jnp.int32)
    large = jnp.minimum(large, half - 1)
    return ret + jnp.where(n < max_exact, n, large)


def hgrn2_mixer(q, f, i, g, lb, out_norm):
    B, S, _ = q.shape
    nc = S // CHUNK
    f32 = jnp.float32
    lbf = lb.astype(f32)
    qf = jax.nn.silu(q.astype(f32))
    fg = lbf + (1.0 - lbf) * jax.nn.sigmoid(f.astype(f32))
    kf = 1.0 - fg
    lg = jnp.log(fg)

    def to_chunks(t):
        return t.reshape(B, nc, CHUNK, A_HEADS, A_HEAD_DIM).transpose(1, 0, 3, 2, 4)

    qc, kc, vc, gc = (to_chunks(t) for t in (qf, kf, i.astype(f32), lg))
    tri = jnp.tril(jnp.ones((CHUNK, CHUNK), dtype=bool))

    def step(state, inp):
        qb, kb, vb, gb = inp
        b = jnp.cumsum(gb, axis=2)
        o_inter = jnp.einsum('bhtk,bhkv->bhtv', qb * jnp.exp(b), state)
        diff = b[:, :, :, None, :] - b[:, :, None, :, :]
        decay = jnp.exp(jnp.where(tri[:, :, None], diff, -jnp.inf))
        att = jnp.einsum('bhtk,bhsk,bhtsk->bhts', qb, kb, decay)
        o_intra = jnp.einsum('bhts,bhsv->bhtv', att, vb)
        b_end = b[:, :, -1:, :]
        new_state = jnp.exp(b_end[:, :, 0, :])[..., None] * state + jnp.einsum('bhsk,bhsv->bhkv', kb * jnp.exp(b_end - b), vb)
        return new_state, o_inter + o_intra

    s0 = jnp.zeros((B, A_HEADS, A_HEAD_DIM, A_HEAD_DIM), f32)
    _, o = lax.scan(step, s0, (qc, kc, vc, gc))
    o = o.transpose(1, 0, 3, 2, 4).reshape(B, S, A_HEADS, A_HEAD_DIM)
    o = o * lax.rsqrt(jnp.mean(o * o, axis=-1, keepdims=True) + EPS) * out_norm.astype(f32)
    o = o.reshape(B, S, A_WIDTH) * jax.nn.silu(g.astype(f32))
    return o.astype(q.dtype)


def dsa_mla_mixer(c_q, c_kv, k_idx, w_idx, q_norm, kv_norm, w_uq, w_uk, w_uv, w_iq, rel_bias):
    B, S, _ = c_q.shape
    f32 = jnp.float32
    top_k = min(TOPK_MAX, S // 4)
    c_q = rms_norm(c_q, q_norm)
    c_kv = rms_norm(c_kv, kv_norm)
    q = (c_q @ w_uq).reshape(B, S, B_HEADS, B_QK_DIM)
    q_lat = jnp.einsum('bshd,hcd->bshc', q, w_uk) * (B_QK_DIM ** -0.5)
    q_idx = (c_q @ w_iq).reshape(B, S, IDX_HEADS, IDX_DIM)
    w_idx = w_idx * ((IDX_HEADS * IDX_DIM) ** -0.5)
    nb = S // Q_BLOCK

    def blk(t):
        return t.reshape(B, nb, Q_BLOCK, *t.shape[2:]).swapaxes(0, 1)

    key_pos = jnp.arange(S, dtype=jnp.int32)
    key_chunk = key_pos // CHUNK
    q_pos_blocks = key_pos.reshape(nb, Q_BLOCK)

    def block(args):
        ql, qi, wi, qpos = args
        qchunk = qpos // CHUNK
        adm = key_chunk[None, :] <= qchunk[:, None]
        score = jax.nn.relu(jnp.einsum('bthd,bsd->bths', qi, k_idx))
        isc = jnp.einsum('bths,bth->bts', score, wi)
        isc = jnp.where(adm[None], isc, -jnp.inf)
        _, sel = lax.top_k(isc, top_k)
        kv_sel = jax.vmap(lambda c, ix: c[ix])(c_kv, sel)
        valid = (sel // CHUNK) <= qchunk[None, :, None]
        bias = rel_bias[t5_bucket(sel - qpos[None, :, None])]
        logits = jnp.einsum('bthc,btkc->bthk', ql, kv_sel).astype(f32) + bias.transpose(0, 1, 3, 2).astype(f32)
        logits = jnp.where(valid[:, :, None, :], logits, -jnp.inf)
        p = jax.nn.softmax(logits, axis=-1).astype(kv_sel.dtype)
        return jnp.einsum('bthk,btkc->bthc', p, kv_sel)

    o_lat = lax.map(block, (blk(q_lat), blk(q_idx), blk(w_idx), q_pos_blocks))
    o_lat = o_lat.swapaxes(0, 1).reshape(B, S, B_HEADS, KV_RANK)
    o = jnp.einsum('bshc,hcv->bshv', o_lat, w_uv)
    return o.reshape(B, S, B_WIDTH)


def conv_ffn(h, w_up, conv_w, conv_b, w_down):
    u = h @ w_up
    S = u.shape[1]
    up = jnp.pad(u, ((0, 0), (CONV_WIDTH - 1, 0), (0, 0)))
    acc = conv_w[CONV_WIDTH - 1] * up[:, CONV_WIDTH - 1:CONV_WIDTH - 1 + S]
    for j in range(CONV_WIDTH - 1):
        acc = acc + conv_w[j] * up[:, j:j + S]
    u = acc + conv_b
    a, v = jnp.split(u, 2, axis=-1)
    return (jax.nn.silu(a) * v) @ w_down


def setup_inputs(seed: int = 0) -> dict:
    key = jax.random.key(seed)
    ks = jax.random.split(key, 24)
    f32 = jnp.float32

    def nrm(k, shape, fan_in):
        return jax.random.normal(k, shape, f32) * (fan_in ** -0.5)

    def gain(k, shape):
        return 1.0 + 0.01 * jax.random.normal(k, shape, f32)

    return {
        'x': jax.random.normal(ks[0], (BATCH, SEQ, D_MODEL), f32),
        'attn_norm': gain(ks[1], (DEPTH, D_MODEL)),
        'w_in': nrm(ks[2], (DEPTH, D_MODEL, N_IN), D_MODEL),
        'lower_bounds': 0.5 * jax.random.normal(ks[3], (DEPTH + 1, A_WIDTH), f32),
        'hgrn_out_norm': gain(ks[4], (DEPTH, A_HEAD_DIM)),
        'q_norm': gain(ks[5], (DEPTH, Q_RANK)),
        'kv_norm': gain(ks[6], (DEPTH, KV_RANK)),
        'w_uq': nrm(ks[7], (DEPTH, Q_RANK, B_HEADS * B_QK_DIM), Q_RANK),
        'w_uk': nrm(ks[8], (DEPTH, B_HEADS, KV_RANK, B_QK_DIM), KV_RANK),
        'w_uv': nrm(ks[9], (DEPTH, B_HEADS, KV_RANK, B_V_DIM), KV_RANK),
        'w_iq': nrm(ks[10], (DEPTH, Q_RANK, IDX_HEADS * IDX_DIM), Q_RANK),
        'rel_bias': 0.2 * jax.random.normal(ks[11], (N_BUCKETS, B_HEADS), f32),
        'w_branch_a': nrm(ks[12], (DEPTH, A_WIDTH, D_MODEL), A_WIDTH),
        'w_branch_b': nrm(ks[13], (DEPTH, B_WIDTH, D_MODEL), B_WIDTH),
        'w_o': nrm(ks[14], (DEPTH, D_MODEL, D_MODEL), D_MODEL),
        'ffn_norm': gain(ks[15], (DEPTH, D_MODEL)),
        'w_up': nrm(ks[16], (DEPTH, D_MODEL, 2 * D_FF), D_MODEL),
        'conv_w': nrm(ks[17], (DEPTH, CONV_WIDTH, 2 * D_FF), CONV_WIDTH),
        'conv_b': 0.01 * jax.random.normal(ks[18], (DEPTH, 2 * D_FF), f32),
        'w_down': nrm(ks[19], (DEPTH, D_FF, D_MODEL), D_FF),
        'final_norm': gain(ks[20], (D_MODEL,)),
    }


def reference(x, attn_norm, w_in, lower_bounds, hgrn_out_norm, q_norm, kv_norm, w_uq, w_uk, w_uv, w_iq, rel_bias, w_branch_a, w_branch_b, w_o, ffn_norm, w_up, conv_w, conv_b, w_down, final_norm):
    lb_table = jnp.cumsum(jax.nn.softmax(lower_bounds.astype(jnp.float32), axis=0), axis=0)
    for l in range(DEPTH):
        h = rms_norm(x, attn_norm[l])
        q_a, f_a, i_a, g_a, c_q, c_kv, k_idx, w_idx, gate_a, gate_b = split_cols(h @ w_in[l])
        y_a = hgrn2_mixer(q_a, f_a, i_a, g_a, lb_table[l], hgrn_out_norm[l]) @ w_branch_a[l]
        y_b = dsa_mla_mixer(c_q, c_kv, k_idx, w_idx, q_norm[l], kv_norm[l], w_uq[l], w_uk[l], w_uv[l], w_iq[l], rel_bias) @ w_branch_b[l]
        mix = jax.nn.sigmoid(gate_a) * y_a + jax.nn.sigmoid(gate_b) * y_b
        x = x + mix @ w_o[l]
        x = x + conv_ffn(rms_norm(x, ffn_norm[l]), w_up[l], conv_w[l], conv_b[l], w_down[l])
    return rms_norm(x, final_norm)
```

```python
import functools
import math

import jax
import jax.numpy as jnp
from jax import lax
from jax.experimental import pallas as pl
from jax.experimental.pallas import tpu as pltpu

F32 = jnp.float32
BF16 = jnp.bfloat16
I32 = jnp.int32

EPS = 1e-6
CHUNK = 64
A_HEADS = 8
A_HEAD_DIM = 128
A_WIDTH = A_HEADS * A_HEAD_DIM
B_HEADS = 16
B_QK_DIM = 64
B_V_DIM = 64
B_WIDTH = B_HEADS * B_V_DIM
Q_RANK = 256
KV_RANK = 128
IDX_HEADS = 8
IDX_DIM = 64
TOPK_MAX = 256
N_BUCKETS = 32
MAX_DISTANCE = 128
CONV_WIDTH = 3

LANES = 128
VMEM_LIMIT = 56 * 1024 * 1024

COL_QFIG = 0
COL_GATE = 4 * A_WIDTH
COL_SMALL = COL_GATE + 2048
SMALL_W = 512
N_PROJ = COL_SMALL + SMALL_W

TQ = 128
TKA = 512
TK = 256
NEG = -1e30
EXP_SAFE = 80.0
MAX_BISECT = 300


def _cparams(sem):
    return pltpu.CompilerParams(dimension_semantics=sem, vmem_limit_bytes=VMEM_LIMIT)


def _const_spec(shape):
    nd = len(shape)
    return pl.BlockSpec(shape, lambda *_: (0,) * nd, pipeline_mode=pl.Buffered(1))


def _dot(a, b):
    return jnp.dot(a, b, preferred_element_type=F32)


def _dot_nt(a, b):
    return lax.dot_general(a, b, (((1,), (1,)), ((), ())), preferred_element_type=F32)


def _dot_tn(a, b):
    return lax.dot_general(a, b, (((0,), (0,)), ((), ())), preferred_element_type=F32)


def _sigmoid(x):
    return jax.nn.sigmoid(x)


def _inproj_kernel(x_ref, g_ref, w_ref, o_ref, h_scr):
    @pl.when(pl.program_id(1) == 0)
    def _():
        x = x_ref[...]
        y = x * lax.rsqrt(jnp.mean(x * x, axis=-1, keepdims=True) + EPS) * g_ref[...]
        h_scr[...] = y.astype(BF16)

    o_ref[...] = _dot(h_scr[...], w_ref[...])


def _inproj(x2, gain, w_pad, tm, tn):
    s, d = x2.shape
    n = w_pad.shape[1]
    return pl.pallas_call(
        _inproj_kernel,
        out_shape=jax.ShapeDtypeStruct((s, n), F32),
        grid=(s // tm, n // tn),
        in_specs=[
            pl.BlockSpec((tm, d), lambda i, j: (i, 0)),
            pl.BlockSpec((1, d), lambda i, j: (0, 0)),
            pl.BlockSpec((d, tn), lambda i, j: (0, j)),
        ],
        out_specs=pl.BlockSpec((tm, tn), lambda i, j: (i, j)),
        scratch_shapes=[pltpu.VMEM((tm, d), BF16)],
        compiler_params=_cparams(("arbitrary", "arbitrary")),
        name="inproj",
    )(x2, gain, w_pad)


def _hgrn_kernel(q_ref, f_ref, i_ref, g_ref, lbp_ref, on_ref, o_ref,
                 st_ref, b_scr, k_scr, qs_scr, att_scr, *, nch):
    c_ = CHUNK
    half = c_ // 2

    @pl.when(pl.program_id(1) == 0)
    def _():
        st_ref[...] = jnp.zeros_like(st_ref)

    l0 = lbp_ref[0:1, :]
    l1 = lbp_ref[1:2, :]
    lm = jnp.maximum(l0, l1)
    e0 = jnp.exp(l0 - lm)
    e1 = jnp.exp(l1 - lm)
    lb = e0 / (e0 + e1)

    row = lax.broadcasted_iota(I32, (c_, c_), 0)
    col = lax.broadcasted_iota(I32, (c_, c_), 1)
    tri = row >= col
    tri_bf = jnp.where(tri, 1.0, 0.0).astype(BF16)

    q = q_ref[...]
    qs_scr[...] = q * _sigmoid(q)
    fg = lb + (1.0 - lb) * _sigmoid(f_ref[...])
    k_scr[...] = 1.0 - fg
    lg = jnp.log(fg)

    dev = jnp.zeros((1, 1), F32)
    for c in range(nch):
        l = lg[c * c_:(c + 1) * c_]
        hi = l.astype(BF16)
        r1 = l - hi.astype(F32)
        mid = r1.astype(BF16)
        lo = (r1 - mid.astype(F32)).astype(BF16)
        b = _dot(tri_bf, hi) + _dot(tri_bf, mid) + _dot(tri_bf, lo)
        b_scr[c * c_:(c + 1) * c_, :] = b
        rho = b[half - 1:half, :]
        dev = jnp.maximum(dev, jnp.max(jnp.abs(b - rho), keepdims=True))
    safe = dev[0, 0] <= EXP_SAFE

    @pl.when(safe)
    def _():
        for c in range(nch):
            sl = slice(c * c_, (c + 1) * c_)
            b = b_scr[sl, :]
            rho = b[half - 1:half, :]
            qt = (qs_scr[sl, :] * jnp.exp(b - rho)).astype(BF16)
            kt = (k_scr[sl, :] * jnp.exp(rho - b)).astype(BF16)
            att_scr[c] = jnp.where(tri, _dot_nt(qt, kt), 0.0)

    @pl.when(jnp.logical_not(safe))
    def _():
        trow = lax.broadcasted_iota(I32, (c_, A_HEAD_DIM), 0)
        for c in range(nch):
            sl = slice(c * c_, (c + 1) * c_)
            b = b_scr[sl, :]
            qv = qs_scr[sl, :]

            def body(s, att):
                bs = b_scr[pl.ds(c * c_ + s, 1), :]
                ks = k_scr[pl.ds(c * c_ + s, 1), :]
                d = jnp.where(trow >= s, b - bs, -jnp.inf)
                colv = jnp.sum(qv * ks * jnp.exp(d), axis=-1, keepdims=True)
                return jnp.where(col == s, colv, att)

            att_scr[c] = lax.fori_loop(0, c_, body, jnp.zeros((c_, c_), F32))

    on = on_ref[...]
    for c in range(nch):
        sl = slice(c * c_, (c + 1) * c_)
        b = b_scr[sl, :]
        bend = b[c_ - 1:c_, :]
        v = i_ref[sl, :].astype(BF16)
        st = st_ref[...]
        qe = (qs_scr[sl, :] * jnp.exp(b)).astype(BF16)
        o = _dot_nt(qe, st.astype(BF16)) + _dot(att_scr[c].astype(BF16), v)
        kd = (k_scr[sl, :] * jnp.exp(bend - b)).astype(BF16)
        st_ref[...] = st * jnp.exp(bend) + _dot_tn(v, kd)
        o = o * lax.rsqrt(jnp.mean(o * o, axis=-1, keepdims=True) + EPS) * on
        g = g_ref[sl, :]
        o_ref[sl, :] = (o * (g * _sigmoid(g))).astype(o_ref.dtype)


def _hgrn(proj, lower_bounds, out_norm, t_rows):
    s = proj.shape[0]
    nch = t_rows // CHUNK
    d = A_HEAD_DIM
    nb = COL_QFIG // d

    def colspec(off):
        return pl.BlockSpec((t_rows, d), lambda h, i, off=off: (i, nb + off * A_HEADS + h))

    return pl.pallas_call(
        functools.partial(_hgrn_kernel, nch=nch),
        out_shape=jax.ShapeDtypeStruct((s, A_WIDTH), BF16),
        grid=(A_HEADS, s // t_rows),
        in_specs=[
            colspec(0), colspec(1), colspec(2), colspec(3),
            pl.BlockSpec((2, d), lambda h, i: (0, h)),
            pl.BlockSpec((1, d), lambda h, i: (0, 0)),
        ],
        out_specs=pl.BlockSpec((t_rows, d), lambda h, i: (i, h)),
        scratch_shapes=[
            pltpu.VMEM((d, d), F32),
            pltpu.VMEM((t_rows, d), F32),
            pltpu.VMEM((t_rows, d), F32),
            pltpu.VMEM((t_rows, d), F32),
            pltpu.VMEM((nch, CHUNK, CHUNK), F32),
        ],
        compiler_params=_cparams(("arbitrary", "arbitrary")),
        name="hgrn",
    )(proj, proj, proj, proj, lower_bounds, out_norm)


def _mla_prep_kernel(p_ref, qn_ref, kn_ref, wuq_ref, wukt_ref, wiq_ref,
                     qlat_ref, qidx_ref, widx_ref, ckvx_ref, ckvt_ref, kidxt_ref):
    p = p_ref[...]
    cq = p[:, 0:Q_RANK]
    cq = cq * lax.rsqrt(jnp.mean(cq * cq, axis=-1, keepdims=True) + EPS) * qn_ref[...]
    cqb = cq.astype(BF16)
    ckv = p[:, Q_RANK:Q_RANK + KV_RANK]
    ckv = ckv * lax.rsqrt(jnp.mean(ckv * ckv, axis=-1, keepdims=True) + EPS) * kn_ref[...]

    q = _dot(cqb, wuq_ref[...])
    scale = B_QK_DIM ** -0.5
    for h in range(B_HEADS):
        qh = q[:, h * B_QK_DIM:(h + 1) * B_QK_DIM].astype(BF16)
        qlat_ref[:, h * KV_RANK:(h + 1) * KV_RANK] = (_dot(qh, wukt_ref[h]) * scale).astype(BF16)
    qidx_ref[...] = _dot(cqb, wiq_ref[...]).astype(BF16)

    rest = p[:, Q_RANK + KV_RANK:]
    widx_ref[...] = rest[:, IDX_DIM:IDX_DIM + IDX_HEADS] * ((IDX_HEADS * IDX_DIM) ** -0.5)
    ckvx_ref[:, 0:KV_RANK] = ckv.astype(BF16)
    ckvx_ref[:, KV_RANK:] = jnp.ones((p.shape[0], KV_RANK), BF16)
    ckvt_ref[...] = ckv.T.astype(BF16)
    kidxt_ref[...] = rest.T[0:IDX_DIM, :].astype(BF16)


def _mla_prep(proj, q_norm, kv_norm, w_uq, w_ukt, w_iq, tm):
    s = proj.shape[0]
    return pl.pallas_call(
        _mla_prep_kernel,
        out_shape=[
            jax.ShapeDtypeStruct((s, B_HEADS * KV_RANK), BF16),
            jax.ShapeDtypeStruct((s, IDX_HEADS * IDX_DIM), BF16),
            jax.ShapeDtypeStruct((s, IDX_HEADS), F32),
            jax.ShapeDtypeStruct((s, 2 * KV_RANK), BF16),
            jax.ShapeDtypeStruct((KV_RANK, s), BF16),
            jax.ShapeDtypeStruct((IDX_DIM, s), BF16),
        ],
        grid=(s // tm,),
        in_specs=[
            pl.BlockSpec((tm, SMALL_W), lambda i: (i, COL_SMALL // SMALL_W)),
            _const_spec((1, Q_RANK)),
            _const_spec((1, KV_RANK)),
            _const_spec((Q_RANK, B_HEADS * B_QK_DIM)),
            _const_spec((B_HEADS, B_QK_DIM, KV_RANK)),
            _const_spec((Q_RANK, IDX_HEADS * IDX_DIM)),
        ],
        out_specs=[
            pl.BlockSpec((tm, B_HEADS * KV_RANK), lambda i: (i, 0)),
            pl.BlockSpec((tm, IDX_HEADS * IDX_DIM), lambda i: (i, 0)),
            pl.BlockSpec((tm, IDX_HEADS), lambda i: (i, 0)),
            pl.BlockSpec((tm, 2 * KV_RANK), lambda i: (i, 0)),
            pl.BlockSpec((KV_RANK, tm), lambda i: (0, i)),
            pl.BlockSpec((IDX_DIM, tm), lambda i: (0, i)),
        ],
        compiler_params=_cparams(("arbitrary",)),
        name="mla_prep",
    )(proj, q_norm, kv_norm, w_uq, w_ukt, w_iq)


def _bias_tab_kernel(bkt_ref, rb_ref, o_ref):
    far = N_BUCKETS // 2 - 1
    for v in range(2):
        bkt = bkt_ref[v]
        for h in range(B_HEADS):
            acc = jnp.zeros(bkt.shape, F32)
            for k in range(N_BUCKETS):
                acc = jnp.where(bkt == k, rb_ref[k, h] - rb_ref[far, h], acc)
            o_ref[v, h] = acc


def _bias_tab(buckets, rel_bias):
    return pl.pallas_call(
        _bias_tab_kernel,
        out_shape=jax.ShapeDtypeStruct((2, B_HEADS, TQ, 2 * TQ), F32),
        in_specs=[
            pl.BlockSpec(memory_space=pltpu.VMEM),
            pl.BlockSpec(memory_space=pltpu.SMEM),
        ],
        out_specs=pl.BlockSpec(memory_space=pltpu.VMEM),
        compiler_params=pltpu.CompilerParams(vmem_limit_bytes=VMEM_LIMIT),
        name="bias_tab",
    )(buckets, rel_bias)


def _t5_bucket(rel):
    half = N_BUCKETS // 2
    max_exact = half // 2
    ret = jnp.where(rel > 0, half, 0)
    n = jnp.abs(rel)
    nf = jnp.maximum(n, 1).astype(jnp.float32)
    large = max_exact + (jnp.log(nf / max_exact) / math.log(MAX_DISTANCE / max_exact) * (half - max_exact)).astype(jnp.int32)
    large = jnp.minimum(large, half - 1)
    return ret + jnp.where(n < max_exact, n, large)


def _dsa_kernel(qlat_ref, qidx_ref, widx_ref, kidxt_ref, ckvt_ref, ckvx_ref, biasd_ref, wuv_ref,
                o_ref, isc_scr, t_scr, lhs_scr, rhs_scr, m_scr, acc_scr, *, top_k):
    qi = pl.program_id(0)
    q0 = qi * TQ
    kf = float(top_k)
    lane = lax.broadcasted_iota(I32, (TQ, LANES), 1)
    rowi = lax.broadcasted_iota(I32, (TQ, LANES), 0)
    lim = q0 + (rowi // CHUNK + 1) * CHUNK

    na = (q0 + TQ + TKA - 1) // TKA
    qidx = qidx_ref[...]
    qh = [qidx[:, h * IDX_DIM:(h + 1) * IDX_DIM] for h in range(IDX_HEADS)]
    w = widx_ref[...]

    def phase_a(jb, carry):
        mn, mx = carry
        k0 = pl.multiple_of(jb * TKA, TKA)
        kt = kidxt_ref[:, pl.ds(k0, TKA)]
        acc = jnp.zeros((TQ, TKA), F32)
        for h in range(IDX_HEADS):
            acc = acc + jnp.maximum(_dot(qh[h], kt), 0.0) * w[:, h:h + 1]
        for g in range(TKA // LANES):
            a = acc[:, g * LANES:(g + 1) * LANES]
            adm = (k0 + g * LANES + lane) < lim
            mn = jnp.minimum(mn, jnp.where(adm, a, jnp.inf))
            mx = jnp.maximum(mx, jnp.where(adm, a, -jnp.inf))
            isc_scr[:, pl.ds(k0 + g * LANES, LANES)] = jnp.where(adm, a, -jnp.inf)
        return mn, mx

    mn, mx = lax.fori_loop(0, na, phase_a,
                           (jnp.full((TQ, LANES), jnp.inf, F32), jnp.full((TQ, LANES), -jnp.inf, F32)))
    ng = na * (TKA // LANES)

    def count(pred):
        def body(g, cnt):
            c0 = pl.multiple_of(g * LANES, LANES)
            x = isc_scr[:, pl.ds(c0, LANES)]
            return cnt + jnp.where(pred(x, c0 + lane), 1.0, 0.0)
        cnt = lax.fori_loop(0, ng, body, jnp.zeros((TQ, LANES), F32))
        return jnp.broadcast_to(jnp.sum(cnt, axis=1, keepdims=True), (TQ, LANES))

    t_scr[...] = jnp.full((TQ, LANES), -jnp.inf, F32)

    @pl.when(q0 + CHUNK > top_k)
    def _():
        rmin = jnp.broadcast_to(jnp.min(mn, axis=1, keepdims=True), (TQ, LANES))
        rmax = jnp.broadcast_to(jnp.max(mx, axis=1, keepdims=True), (TQ, LANES))
        lo0 = rmin
        hi0 = rmax + (rmax - rmin) + jnp.abs(rmax) * 1e-3 + 1e-30

        def midpoint(lo, hi):
            return 0.5 * lo + 0.5 * hi

        def n_active(lo, hi, done):
            mid = midpoint(lo, hi)
            act = jnp.logical_and(done == 0.0, jnp.logical_and(mid > lo, mid < hi))
            return jnp.sum(jnp.where(act, 1.0, 0.0))

        def cond(st):
            it, nact, lo, hi, done = st
            return jnp.logical_and(it < MAX_BISECT, nact > 0.0)

        def body(st):
            it, nact, lo, hi, done = st
            mid = midpoint(lo, hi)
            act = jnp.logical_and(done == 0.0, jnp.logical_and(mid > lo, mid < hi))
            c = count(lambda x, idx: x >= mid)
            ge = c >= kf
            lo = jnp.where(jnp.logical_and(act, ge), mid, lo)
            hi = jnp.where(jnp.logical_and(act, jnp.logical_not(ge)), mid, hi)
            done = jnp.where(jnp.logical_and(act, c == kf), 1.0, done)
            return it + 1, n_active(lo, hi, done), lo, hi, done

        done0 = jnp.zeros((TQ, LANES), F32)
        _, _, lo, hi, done = lax.while_loop(
            cond, body, (jnp.int32(0), n_active(lo0, hi0, done0), lo0, hi0, done0))
        t_scr[...] = lo

        tie = done == 0.0
        n_tie = jnp.sum(jnp.where(tie, 1.0, 0.0))

        @pl.when(n_tie > 0.0)
        def _():
            r = kf - count(lambda x, idx: x >= hi)
            p = jnp.zeros((TQ, LANES), I32)
            nbits = max(1, (isc_scr.shape[1] - 1).bit_length())
            for bit in range(nbits - 1, -1, -1):
                cand = p + (1 << bit)
                f = count(lambda x, idx: jnp.logical_and(jnp.logical_and(x >= lo, x < hi), idx < cand))
                p = jnp.where(f < r, cand, p)

            def drop(g, carry):
                c0 = pl.multiple_of(g * LANES, LANES)
                x = isc_scr[:, pl.ds(c0, LANES)]
                ex = jnp.logical_and(jnp.logical_and(tie, (c0 + lane) > p),
                                     jnp.logical_and(x >= lo, x < hi))
                isc_scr[:, pl.ds(c0, LANES)] = jnp.where(ex, -jnp.inf, x)
                return carry

            lax.fori_loop(0, ng, drop, 0)

    thr = t_scr[...]
    thr2 = jnp.concatenate([thr, thr], axis=1)
    lane2 = lax.broadcasted_iota(I32, (TQ, TK), 1)
    lim2 = jnp.concatenate([lim, lim], axis=1)
    for h in range(B_HEADS):
        lhs_scr[h * TQ:(h + 1) * TQ, 0:KV_RANK] = qlat_ref[:, h * KV_RANK:(h + 1) * KV_RANK]

    @pl.when(qi == 0)
    def _():
        eye = jnp.where(lax.broadcasted_iota(I32, (TQ, TQ), 0) == lax.broadcasted_iota(I32, (TQ, TQ), 1),
                        1.0, 0.0).astype(BF16)
        for h in range(B_HEADS):
            lhs_scr[h * TQ:(h + 1) * TQ, KV_RANK:] = eye

    m_scr[...] = jnp.full(m_scr.shape, NEG, F32)
    acc_scr[...] = jnp.zeros_like(acc_scr)

    def attend(k0, sel, bias_v):
        rhs_scr[0:KV_RANK, :] = ckvt_ref[:, pl.ds(k0, TK)]
        rhs_scr[KV_RANK:, :] = jnp.where(sel, 0.0, NEG).astype(BF16)
        rhs = rhs_scr[...]
        kv = ckvx_ref[pl.ds(k0, TK), :]
        for h in range(B_HEADS):
            rs = slice(h * TQ, (h + 1) * TQ)
            s = _dot(lhs_scr[rs, :], rhs)
            if bias_v is not None:
                s = s + biasd_ref[0, h]
            m_old = m_scr[rs, :]
            m_new = jnp.maximum(m_old, jnp.max(s, axis=1, keepdims=True))
            alpha = jnp.exp(m_old - m_new)
            p = jnp.exp(s - jnp.concatenate([m_new, m_new], axis=1))
            acc_scr[rs, :] = acc_scr[rs, :] * jnp.concatenate([alpha, alpha], axis=1) + _dot(p.astype(BF16), kv)
            m_scr[rs, :] = m_new

    far_lim = jnp.maximum(q0 - TQ, 0)
    nf = (far_lim + TK - 1) // TK

    def far_block(jb, carry):
        k0 = pl.multiple_of(jb * TK, TK)
        x = isc_scr[:, pl.ds(k0, TK)]
        sel = jnp.logical_and(x >= thr2, (k0 + lane2) < far_lim)
        attend(k0, sel, None)
        return carry

    lax.fori_loop(0, nf, far_block, 0)

    k0n = pl.multiple_of(far_lim, TQ)
    xn = isc_scr[:, pl.ds(k0n, TK)]
    seln = jnp.logical_and(xn >= thr2, (k0n + lane2) < lim2)
    attend(k0n, seln, True)

    for h in range(B_HEADS):
        rs = slice(h * TQ, (h + 1) * TQ)
        a = acc_scr[rs, :]
        olat = (a[:, 0:KV_RANK] / a[:, KV_RANK:]).astype(BF16)
        o_ref[:, h * B_V_DIM:(h + 1) * B_V_DIM] = _dot(olat, wuv_ref[h]).astype(o_ref.dtype)


def _dsa(qlat, qidx, widx, kidxt, ckvt, ckvx, biasd, w_uv, top_k):
    s = qlat.shape[0]
    assert s % TKA == 0 and top_k >= CHUNK
    return pl.pallas_call(
        functools.partial(_dsa_kernel, top_k=top_k),
        out_shape=jax.ShapeDtypeStruct((s, B_WIDTH), BF16),
        grid=(s // TQ,),
        in_specs=[
            pl.BlockSpec((TQ, B_HEADS * KV_RANK), lambda i: (i, 0)),
            pl.BlockSpec((TQ, IDX_HEADS * IDX_DIM), lambda i: (i, 0)),
            pl.BlockSpec((TQ, IDX_HEADS), lambda i: (i, 0)),
            _const_spec((IDX_DIM, s)),
            _const_spec((KV_RANK, s)),
            _const_spec((s, 2 * KV_RANK)),
            pl.BlockSpec((1, B_HEADS, TQ, TK), lambda i: (jnp.minimum(i, 1), 0, 0, 0)),
            _const_spec((B_HEADS, KV_RANK, B_V_DIM)),
        ],
        out_specs=pl.BlockSpec((TQ, B_WIDTH), lambda i: (i, 0)),
        scratch_shapes=[
            pltpu.VMEM((TQ, s), F32),
            pltpu.VMEM((TQ, LANES), F32),
            pltpu.VMEM((B_HEADS * TQ, 2 * KV_RANK), BF16),
            pltpu.VMEM((2 * KV_RANK, TK), BF16),
            pltpu.VMEM((B_HEADS * TQ, LANES), F32),
            pltpu.VMEM((B_HEADS * TQ, 2 * KV_RANK), F32),
        ],
        compiler_params=_cparams(("arbitrary",)),
        name="dsa",
    )(qlat, qidx, widx, kidxt, ckvt, ckvx, biasd, w_uv)


def _merge_kernel(x_ref, hg_ref, ob_ref, ga_ref, gb_ref, wa_ref, wb_ref, wo_ref, fn_ref,
                  x1_ref, h2_ref):
    ya = _dot(hg_ref[...], wa_ref[...])
    yb = _dot(ob_ref[...], wb_ref[...])
    mix = _sigmoid(ga_ref[...]) * ya + _sigmoid(gb_ref[...]) * yb
    x1 = x_ref[...] + _dot(mix.astype(BF16), wo_ref[...])
    x1_ref[...] = x1
    h2 = x1 * lax.rsqrt(jnp.mean(x1 * x1, axis=-1, keepdims=True) + EPS) * fn_ref[...]
    h2_ref[...] = h2.astype(BF16)


def _merge(x2, hg, ob, proj, w_a, w_b, w_o, ffn_norm, tm):
    s, d = x2.shape
    gblk = COL_GATE // d
    row = lambda i: (i, 0)
    return pl.pallas_call(
        _merge_kernel,
        out_shape=[jax.ShapeDtypeStruct((s, d), F32), jax.ShapeDtypeStruct((s, d), BF16)],
        grid=(s // tm,),
        in_specs=[
            pl.BlockSpec((tm, d), row),
            pl.BlockSpec((tm, A_WIDTH), row),
            pl.BlockSpec((tm, B_WIDTH), row),
            pl.BlockSpec((tm, d), lambda i: (i, gblk)),
            pl.BlockSpec((tm, d), lambda i: (i, gblk + 1)),
            _const_spec((A_WIDTH, d)),
            _const_spec((B_WIDTH, d)),
            _const_spec((d, d)),
            _const_spec((1, d)),
        ],
        out_specs=[pl.BlockSpec((tm, d), row), pl.BlockSpec((tm, d), row)],
        compiler_params=_cparams(("arbitrary",)),
        name="merge",
    )(x2, hg, ob, proj, proj, w_a, w_b, w_o, ffn_norm)


def _ffn_kernel(x1_ref, h2_ref, wup_ref, cw_ref, cb_ref, wdn_ref, fin_ref, o_ref,
                tail_scr, u_scr, act_scr, *, cbw):
    tm = h2_ref.shape[0]
    dff = wdn_ref.shape[0]
    halo = 8

    @pl.when(pl.program_id(0) == 0)
    def _():
        tail_scr[...] = jnp.zeros_like(tail_scr)

    h2 = h2_ref[...]

    def conv_cols(c0):
        u = _dot(h2, wup_ref[:, c0:c0 + cbw])
        u_scr[0:halo, :] = tail_scr[:, c0:c0 + cbw]
        u_scr[halo:, :] = u
        tail_scr[:, c0:c0 + cbw] = u[tm - halo:, :]
        cw = cw_ref[:, c0:c0 + cbw]
        return (cw[2:3] * u + cw[1:2] * u_scr[halo - 1:halo - 1 + tm, :]
                + cw[0:1] * u_scr[halo - 2:halo - 2 + tm, :] + cb_ref[:, c0:c0 + cbw])

    for j in range(dff // cbw):
        a = conv_cols(j * cbw)
        v = conv_cols(dff + j * cbw)
        act_scr[:, j * cbw:(j + 1) * cbw] = (a * _sigmoid(a) * v).astype(BF16)

    y = x1_ref[...] + _dot(act_scr[...], wdn_ref[...])
    o_ref[...] = y * lax.rsqrt(jnp.mean(y * y, axis=-1, keepdims=True) + EPS) * fin_ref[...]


def _ffn(x1, h2, w_up, conv_w, conv_b, w_down, final_norm, tm, cbw):
    s, d = x1.shape
    dff = w_down.shape[0]
    row = lambda i: (i, 0)
    return pl.pallas_call(
        functools.partial(_ffn_kernel, cbw=cbw),
        out_shape=jax.ShapeDtypeStruct((s, d), F32),
        grid=(s // tm,),
        in_specs=[
            pl.BlockSpec((tm, d), row),
            pl.BlockSpec((tm, d), row),
            _const_spec((d, 2 * dff)),
            _const_spec((CONV_WIDTH, 2 * dff)),
            _const_spec((1, 2 * dff)),
            _const_spec((dff, d)),
            _const_spec((1, d)),
        ],
        out_specs=pl.BlockSpec((tm, d), row),
        scratch_shapes=[
            pltpu.VMEM((8, 2 * dff), F32),
            pltpu.VMEM((tm + 8, cbw), F32),
            pltpu.VMEM((tm, dff), BF16),
        ],
        compiler_params=_cparams(("arbitrary",)),
        name="ffn",
    )(x1, h2, w_up, conv_w, conv_b, w_down, final_norm)


def _row_tile(s, want):
    t = min(want, s)
    while s % t:
        t //= 2
    return t


def kernel(x, attn_norm, w_in, lower_bounds, hgrn_out_norm, q_norm, kv_norm, w_uq, w_uk, w_uv, w_iq,
           rel_bias, w_branch_a, w_branch_b, w_o, ffn_norm, w_up, conv_w, conv_b, w_down, final_norm):
    bsz, s, d = x.shape
    assert bsz == 1 and attn_norm.shape[0] == 1 and lower_bounds.shape[0] == 2
    x2 = x.reshape(s, d)
    top_k = min(TOPK_MAX, s // 4)

    wi = w_in[0]
    n_small = Q_RANK + KV_RANK + IDX_DIM + IDX_HEADS
    w_pad = jnp.concatenate(
        [wi[:, :COL_GATE], wi[:, COL_GATE + n_small:], wi[:, COL_GATE:COL_GATE + n_small],
         jnp.zeros((d, SMALL_W - n_small), wi.dtype)], axis=1).astype(BF16)

    proj = _inproj(x2, attn_norm, w_pad, _row_tile(s, 1024), N_PROJ // 4)

    hg = _hgrn(proj, lower_bounds, hgrn_out_norm, _row_tile(s, 512))

    qlat, qidx, widx, ckvx, ckvt, kidxt = _mla_prep(
        proj, q_norm, kv_norm, w_uq[0].astype(BF16), jnp.swapaxes(w_uk[0], 1, 2).astype(BF16),
        w_iq[0].astype(BF16), _row_tile(s, 512))

    t_loc = jnp.arange(TQ, dtype=jnp.int32)[None, :, None]
    s_loc = jnp.arange(TK, dtype=jnp.int32)[None, None, :]
    off = jnp.array([0, -TQ], jnp.int32)[:, None, None]
    buckets = _t5_bucket(s_loc - t_loc + off)
    biasd = _bias_tab(buckets, rel_bias)

    ob = _dsa(qlat, qidx, widx, kidxt, ckvt, ckvx, biasd, w_uv[0].astype(BF16), top_k)

    x1, h2 = _merge(x2, hg, ob, proj, w_branch_a[0].astype(BF16), w_branch_b[0].astype(BF16),
                    w_o[0].astype(BF16), ffn_norm, _row_tile(s, 512))

    out = _ffn(x1, h2, w_up[0].astype(BF16), conv_w[0], conv_b, w_down[0].astype(BF16),
               final_norm.reshape(1, d), _row_tile(s, 512), 256)
    return out.reshape(bsz, s, d)
```

```python
import functools
import math

import jax
import jax.numpy as jnp
from jax import lax
from jax.experimental import pallas as pl
from jax.experimental.pallas import tpu as pltpu

F32 = jnp.float32
BF16 = jnp.bfloat16
I32 = jnp.int32

EPS = 1e-6
CHUNK = 64
A_HEADS = 8
A_HEAD_DIM = 128
A_WIDTH = A_HEADS * A_HEAD_DIM
B_HEADS = 16
B_QK_DIM = 64
B_V_DIM = 64
B_WIDTH = B_HEADS * B_V_DIM
Q_RANK = 256
KV_RANK = 128
IDX_HEADS = 8
IDX_DIM = 64
TOPK_MAX = 256
N_BUCKETS = 32
MAX_DISTANCE = 128
CONV_WIDTH = 3

LANES = 128
VMEM_LIMIT = 56 * 1024 * 1024

COL_QFIG = 0
COL_GATE = 4 * A_WIDTH
COL_SMALL = COL_GATE + 2048
SMALL_W = 512
N_PROJ = COL_SMALL + SMALL_W

TQ = 128
TKA = 512
TK = 512
TKN = 2 * TQ
NEG = -1e30
EXP_SAFE = 80.0
MAX_BISECT = 1024
LOG2E = 1.4426950408889634


def _cparams(sem):
    return pltpu.CompilerParams(dimension_semantics=sem, vmem_limit_bytes=VMEM_LIMIT)


def _const_spec(shape):
    nd = len(shape)
    return pl.BlockSpec(shape, lambda *_: (0,) * nd, pipeline_mode=pl.Buffered(1))


def _dot(a, b):
    return jnp.dot(a, b, preferred_element_type=F32)


def _dot_nt(a, b):
    return lax.dot_general(a, b, (((1,), (1,)), ((), ())), preferred_element_type=F32)


def _dot_tn(a, b):
    return lax.dot_general(a, b, (((0,), (0,)), ((), ())), preferred_element_type=F32)


def _sigmoid(x):
    return jax.nn.sigmoid(x)


def _inproj_kernel(x_ref, g_ref, w_ref, o_ref, h_scr):
    @pl.when(pl.program_id(1) == 0)
    def _():
        x = x_ref[...]
        y = x * lax.rsqrt(jnp.mean(x * x, axis=-1, keepdims=True) + EPS) * g_ref[...]
        h_scr[...] = y.astype(BF16)

    o_ref[...] = _dot(h_scr[...], w_ref[...])


def _inproj(x2, gain, w_pad, tm, tn):
    s, d = x2.shape
    n = w_pad.shape[1]
    return pl.pallas_call(
        _inproj_kernel,
        out_shape=jax.ShapeDtypeStruct((s, n), F32),
        grid=(s // tm, n // tn),
        in_specs=[
            pl.BlockSpec((tm, d), lambda i, j: (i, 0)),
            pl.BlockSpec((1, d), lambda i, j: (0, 0)),
            pl.BlockSpec((d, tn), lambda i, j: (0, j)),
        ],
        out_specs=pl.BlockSpec((tm, tn), lambda i, j: (i, j)),
        scratch_shapes=[pltpu.VMEM((tm, d), BF16)],
        compiler_params=_cparams(("arbitrary", "arbitrary")),
        name="inproj",
    )(x2, gain, w_pad)


def _hgrn_kernel(q_ref, f_ref, i_ref, g_ref, lbp_ref, on_ref, o_ref,
                 st_ref, b_scr, k_scr, qs_scr, att_scr, *, nch):
    c_ = CHUNK
    half = c_ // 2

    @pl.when(pl.program_id(1) == 0)
    def _():
        st_ref[...] = jnp.zeros_like(st_ref)

    l0 = lbp_ref[0:1, :]
    l1 = lbp_ref[1:2, :]
    lm = jnp.maximum(l0, l1)
    e0 = jnp.exp(l0 - lm)
    e1 = jnp.exp(l1 - lm)
    lb = e0 / (e0 + e1)

    row = lax.broadcasted_iota(I32, (c_, c_), 0)
    col = lax.broadcasted_iota(I32, (c_, c_), 1)
    tri = row >= col
    tri_bf = jnp.where(tri, 1.0, 0.0).astype(BF16)

    q = q_ref[...]
    qs_scr[...] = q * _sigmoid(q)
    fg = lb + (1.0 - lb) * _sigmoid(f_ref[...])
    k_scr[...] = 1.0 - fg
    lg = jnp.log(fg)

    dev = jnp.zeros((1, 1), F32)
    for c in range(nch):
        l = lg[c * c_:(c + 1) * c_]
        hi = l.astype(BF16)
        r1 = l - hi.astype(F32)
        mid = r1.astype(BF16)
        lo = (r1 - mid.astype(F32)).astype(BF16)
        b = _dot(tri_bf, hi) + _dot(tri_bf, mid) + _dot(tri_bf, lo)
        b_scr[c * c_:(c + 1) * c_, :] = b
        rho = b[half - 1:half, :]
        dev = jnp.maximum(dev, jnp.max(jnp.abs(b - rho), keepdims=True))
    safe = dev[0, 0] <= EXP_SAFE

    @pl.when(safe)
    def _():
        for c in range(nch):
            sl = slice(c * c_, (c + 1) * c_)
            b = b_scr[sl, :]
            rho = b[half - 1:half, :]
            qt = (qs_scr[sl, :] * jnp.exp(b - rho)).astype(BF16)
            kt = (k_scr[sl, :] * jnp.exp(rho - b)).astype(BF16)
            att_scr[c] = jnp.where(tri, _dot_nt(qt, kt), 0.0)

    @pl.when(jnp.logical_not(safe))
    def _():
        trow = lax.broadcasted_iota(I32, (c_, A_HEAD_DIM), 0)
        for c in range(nch):
            sl = slice(c * c_, (c + 1) * c_)
            b = b_scr[sl, :]
            qv = qs_scr[sl, :]

            def body(s, att):
                bs = b_scr[pl.ds(c * c_ + s, 1), :]
                ks = k_scr[pl.ds(c * c_ + s, 1), :]
                d = jnp.where(trow >= s, b - bs, -jnp.inf)
                colv = jnp.sum(qv * ks * jnp.exp(d), axis=-1, keepdims=True)
                return jnp.where(col == s, colv, att)

            att_scr[c] = lax.fori_loop(0, c_, body, jnp.zeros((c_, c_), F32))

    on = on_ref[...]
    for c in range(nch):
        sl = slice(c * c_, (c + 1) * c_)
        b = b_scr[sl, :]
        bend = b[c_ - 1:c_, :]
        v = i_ref[sl, :].astype(BF16)
        st = st_ref[...]
        qe = (qs_scr[sl, :] * jnp.exp(b)).astype(BF16)
        o = _dot_nt(qe, st.astype(BF16)) + _dot(att_scr[c].astype(BF16), v)
        kd = (k_scr[sl, :] * jnp.exp(bend - b)).astype(BF16)
        st_ref[...] = st * jnp.exp(bend) + _dot_tn(v, kd)
        o = o * lax.rsqrt(jnp.mean(o * o, axis=-1, keepdims=True) + EPS) * on
        g = g_ref[sl, :]
        o_ref[sl, :] = (o * (g * _sigmoid(g))).astype(o_ref.dtype)


def _hgrn(proj, lower_bounds, out_norm, t_rows):
    s = proj.shape[0]
    nch = t_rows // CHUNK
    d = A_HEAD_DIM
    nb = COL_QFIG // d

    def colspec(off):
        return pl.BlockSpec((t_rows, d), lambda h, i, off=off: (i, nb + off * A_HEADS + h))

    return pl.pallas_call(
        functools.partial(_hgrn_kernel, nch=nch),
        out_shape=jax.ShapeDtypeStruct((s, A_WIDTH), BF16),
        grid=(A_HEADS, s // t_rows),
        in_specs=[
            colspec(0), colspec(1), colspec(2), colspec(3),
            pl.BlockSpec((2, d), lambda h, i: (0, h)),
            pl.BlockSpec((1, d), lambda h, i: (0, 0)),
        ],
        out_specs=pl.BlockSpec((t_rows, d), lambda h, i: (i, h)),
        scratch_shapes=[
            pltpu.VMEM((d, d), F32),
            pltpu.VMEM((t_rows, d), F32),
            pltpu.VMEM((t_rows, d), F32),
            pltpu.VMEM((t_rows, d), F32),
            pltpu.VMEM((nch, CHUNK, CHUNK), F32),
        ],
        compiler_params=_cparams(("arbitrary", "arbitrary")),
        name="hgrn",
    )(proj, proj, proj, proj, lower_bounds, out_norm)


def _mla_prep_kernel(p_ref, qn_ref, kn_ref, wuq_ref, wukt_ref, wiq_ref,
                     qlat_ref, qidx_ref, widx_ref, ckvx_ref, ckvt_ref, kidxt_ref):
    p = p_ref[...]
    cq = p[:, 0:Q_RANK]
    cq = cq * lax.rsqrt(jnp.mean(cq * cq, axis=-1, keepdims=True) + EPS) * qn_ref[...]
    cqb = cq.astype(BF16)
    ckv = p[:, Q_RANK:Q_RANK + KV_RANK]
    ckv = ckv * lax.rsqrt(jnp.mean(ckv * ckv, axis=-1, keepdims=True) + EPS) * kn_ref[...]

    q = _dot(cqb, wuq_ref[...])
    scale = B_QK_DIM ** -0.5 * LOG2E
    for h in range(B_HEADS):
        qh = q[:, h * B_QK_DIM:(h + 1) * B_QK_DIM].astype(BF16)
        qlat_ref[:, h * KV_RANK:(h + 1) * KV_RANK] = (_dot(qh, wukt_ref[h]) * scale).astype(BF16)
    qidx_ref[...] = _dot(cqb, wiq_ref[...]).astype(BF16)

    rest = p[:, Q_RANK + KV_RANK:]
    widx_ref[...] = rest[:, IDX_DIM:IDX_DIM + IDX_HEADS] * ((IDX_HEADS * IDX_DIM) ** -0.5)
    ckvx_ref[:, 0:KV_RANK] = ckv.astype(BF16)
    ckvx_ref[:, KV_RANK:] = jnp.ones((p.shape[0], KV_RANK), BF16)
    ckvt_ref[...] = ckv.T.astype(BF16)
    kidxt_ref[...] = rest.T[0:IDX_DIM, :].astype(BF16)


def _mla_prep(proj, q_norm, kv_norm, w_uq, w_ukt, w_iq, tm):
    s = proj.shape[0]
    return pl.pallas_call(
        _mla_prep_kernel,
        out_shape=[
            jax.ShapeDtypeStruct((s, B_HEADS * KV_RANK), BF16),
            jax.ShapeDtypeStruct((s, IDX_HEADS * IDX_DIM), BF16),
            jax.ShapeDtypeStruct((s, IDX_HEADS), F32),
            jax.ShapeDtypeStruct((s, 2 * KV_RANK), BF16),
            jax.ShapeDtypeStruct((KV_RANK, s), BF16),
            jax.ShapeDtypeStruct((IDX_DIM, s), BF16),
        ],
        grid=(s // tm,),
        in_specs=[
            pl.BlockSpec((tm, SMALL_W), lambda i: (i, COL_SMALL // SMALL_W)),
            _const_spec((1, Q_RANK)),
            _const_spec((1, KV_RANK)),
            _const_spec((Q_RANK, B_HEADS * B_QK_DIM)),
            _const_spec((B_HEADS, B_QK_DIM, KV_RANK)),
            _const_spec((Q_RANK, IDX_HEADS * IDX_DIM)),
        ],
        out_specs=[
            pl.BlockSpec((tm, B_HEADS * KV_RANK), lambda i: (i, 0)),
            pl.BlockSpec((tm, IDX_HEADS * IDX_DIM), lambda i: (i, 0)),
            pl.BlockSpec((tm, IDX_HEADS), lambda i: (i, 0)),
            pl.BlockSpec((tm, 2 * KV_RANK), lambda i: (i, 0)),
            pl.BlockSpec((KV_RANK, tm), lambda i: (0, i)),
            pl.BlockSpec((IDX_DIM, tm), lambda i: (0, i)),
        ],
        compiler_params=_cparams(("arbitrary",)),
        name="mla_prep",
    )(proj, q_norm, kv_norm, w_uq, w_ukt, w_iq)


def _bias_tab_kernel(bkt_ref, rb_ref, o_ref):
    far = N_BUCKETS // 2 - 1
    for v in range(2):
        bkt = bkt_ref[v]
        for h in range(B_HEADS):
            acc = jnp.zeros(bkt.shape, F32)
            for k in range(N_BUCKETS):
                acc = jnp.where(bkt == k, (rb_ref[k, h] - rb_ref[far, h]) * LOG2E, acc)
            o_ref[v, h] = acc


def _bias_tab(buckets, rel_bias):
    return pl.pallas_call(
        _bias_tab_kernel,
        out_shape=jax.ShapeDtypeStruct((2, B_HEADS, TQ, TKN), F32),
        in_specs=[
            pl.BlockSpec(memory_space=pltpu.VMEM),
            pl.BlockSpec(memory_space=pltpu.SMEM),
        ],
        out_specs=pl.BlockSpec(memory_space=pltpu.VMEM),
        compiler_params=pltpu.CompilerParams(vmem_limit_bytes=VMEM_LIMIT),
        name="bias_tab",
    )(buckets, rel_bias)


def _t5_bucket(rel):
    half = N_BUCKETS // 2
    max_exact = half // 2
    ret = jnp.where(rel > 0, half, 0)
    n = jnp.abs(rel)
    nf = jnp.maximum(n, 1).astype(jnp.float32)
    large = max_exact + (jnp.log(nf / max_exact) / math.log(MAX_DISTANCE / max_exact) * (half - max_exact)).astype(jnp.int32)
    large = jnp.minimum(large, half - 1)
    return ret + jnp.where(n < max_exact, n, large)


def _dsa_kernel(qlat_ref, qidx_ref, widx_ref, kidxt_ref, ckvt_ref, ckvx_ref, biasd_ref, wuv_ref,
                o_ref, isc_scr, t_scr, lhs_scr, rhs_scr, m_scr, acc_scr, *, top_k):
    qi = pl.program_id(0)
    q0 = qi * TQ
    kf = float(top_k)
    lane = lax.broadcasted_iota(I32, (TQ, LANES), 1)
    rowi = lax.broadcasted_iota(I32, (TQ, LANES), 0)
    lim = q0 + (rowi // CHUNK + 1) * CHUNK

    na = (q0 + TQ + TKA - 1) // TKA
    qidx = qidx_ref[...]
    qh = [qidx[:, h * IDX_DIM:(h + 1) * IDX_DIM] for h in range(IDX_HEADS)]
    w = widx_ref[...]

    def phase_a(jb, carry):
        mn, mx = carry
        k0 = pl.multiple_of(jb * TKA, TKA)
        kt = kidxt_ref[:, pl.ds(k0, TKA)]
        acc = jnp.zeros((TQ, TKA), F32)
        for h in range(IDX_HEADS):
            acc = acc + jnp.maximum(_dot(qh[h], kt), 0.0) * w[:, h:h + 1]
        for g in range(TKA // LANES):
            a = acc[:, g * LANES:(g + 1) * LANES]
            adm = (k0 + g * LANES + lane) < lim
            mn = jnp.minimum(mn, jnp.where(adm, a, jnp.inf))
            mx = jnp.maximum(mx, jnp.where(adm, a, -jnp.inf))
            isc_scr[:, pl.ds(k0 + g * LANES, LANES)] = jnp.where(adm, a, -jnp.inf)
        return mn, mx

    mn, mx = lax.fori_loop(0, na, phase_a,
                           (jnp.full((TQ, LANES), jnp.inf, F32), jnp.full((TQ, LANES), -jnp.inf, F32)))
    ng = na * (TKA // LANES)

    def count(pred):
        def body(g, cnt):
            c0 = pl.multiple_of(g * LANES, LANES)
            x = isc_scr[:, pl.ds(c0, LANES)]
            return cnt + jnp.where(pred(x, c0 + lane), 1.0, 0.0)
        cnt = lax.fori_loop(0, ng, body, jnp.zeros((TQ, LANES), F32))
        return jnp.broadcast_to(jnp.sum(cnt, axis=1, keepdims=True), (TQ, LANES))

    t_scr[...] = jnp.full((TQ, LANES), -jnp.inf, F32)

    @pl.when(q0 + CHUNK > top_k)
    def _():
        rmin = jnp.broadcast_to(jnp.min(mn, axis=1, keepdims=True), (TQ, LANES))
        rmax = jnp.broadcast_to(jnp.max(mx, axis=1, keepdims=True), (TQ, LANES))
        def count_zero():
            def body(g, cs):
                c0 = pl.multiple_of(g * LANES, LANES)
                x = isc_scr[:, pl.ds(c0, LANES)]
                return cs[0] + jnp.where(x >= 0.0, 1.0, 0.0), cs[1] + jnp.where(x > 0.0, 1.0, 0.0)
            z = jnp.zeros((TQ, LANES), F32)
            ca, cb = lax.fori_loop(0, ng, body, (z, z))
            return (jnp.broadcast_to(jnp.sum(ca, axis=1, keepdims=True), (TQ, LANES)),
                    jnp.broadcast_to(jnp.sum(cb, axis=1, keepdims=True), (TQ, LANES)))

        c_ge0, c_gt0 = count_zero()
        above = c_gt0 >= kf
        below = c_ge0 < kf
        tiny = float(jnp.finfo(F32).tiny)
        hi_top = rmax + (rmax - rmin) + jnp.abs(rmax) * 1e-3 + 1e-30
        lo0 = jnp.where(below, rmin, 0.0)
        clo0 = jnp.where(below, lim.astype(F32), c_ge0)
        hi0 = jnp.where(below, 0.0, jnp.where(above, hi_top, tiny))
        chi0 = jnp.where(below, c_ge0, jnp.where(above, 0.0, c_gt0))
        log_k = math.log(kf)

        def midpoint(lo, hi):
            return 0.5 * lo + 0.5 * hi

        def is_open(lo, hi):
            mid = midpoint(lo, hi)
            return jnp.logical_and(mid > lo, mid < hi)

        def n_active(lo, hi, done):
            act = jnp.logical_and(done == 0.0, is_open(lo, hi))
            return jnp.sum(jnp.where(act, 1.0, 0.0))

        def pivot(it, lo, hi, clo, chi):
            llo = jnp.log(clo)
            lhi = jnp.log(jnp.maximum(chi, 0.5))
            t = jnp.clip((llo - log_k) / (llo - lhi), 0.03, 0.97)
            guess = lo + t * (hi - lo)
            mid = midpoint(lo, hi)
            use = jnp.logical_and(jnp.logical_and(guess > lo, guess < hi), it % 3 != 2)
            return jnp.where(use, guess, mid)

        def count_ge(mid):
            w = 4 * LANES
            mid_w = jnp.concatenate([mid] * 4, axis=1)

            def body(g, cnt):
                c0 = pl.multiple_of(g * w, w)
                ind = jnp.where(isc_scr[:, pl.ds(c0, w)] >= mid_w, 1.0, 0.0)
                return cnt + ((ind[:, 0:LANES] + ind[:, LANES:2 * LANES])
                              + (ind[:, 2 * LANES:3 * LANES] + ind[:, 3 * LANES:]))

            cnt = lax.fori_loop(0, ng // 4, body, jnp.zeros((TQ, LANES), F32))
            return jnp.broadcast_to(jnp.sum(cnt, axis=1, keepdims=True), (TQ, LANES))

        def cond(st):
            it, nact = st[0], st[1]
            return jnp.logical_and(it < MAX_BISECT, nact > 0.0)

        def body(st):
            it, nact, lo, hi, clo, chi, done = st
            act = jnp.logical_and(done == 0.0, is_open(lo, hi))
            mid = pivot(it, lo, hi, clo, chi)
            c = count_ge(mid)
            up = jnp.logical_and(act, c >= kf)
            dn = jnp.logical_and(act, c < kf)
            lo = jnp.where(up, mid, lo)
            clo = jnp.where(up, c, clo)
            hi = jnp.where(dn, mid, hi)
            chi = jnp.where(dn, c, chi)
            done = jnp.where(jnp.logical_and(act, c == kf), 1.0, done)
            return it + 1, n_active(lo, hi, done), lo, hi, clo, chi, done

        done0 = jnp.where(c_ge0 == kf, 1.0, 0.0)
        _, _, lo, hi, _, _, done = lax.while_loop(
            cond, body, (jnp.int32(0), n_active(lo0, hi0, done0), lo0, hi0, clo0, chi0, done0))
        t_scr[...] = lo

        tie = done == 0.0
        n_tie = jnp.sum(jnp.where(tie, 1.0, 0.0))

        @pl.when(n_tie > 0.0)
        def _():
            r = kf - count(lambda x, idx: x >= hi)
            p = jnp.zeros((TQ, LANES), I32)
            nbits = max(1, (isc_scr.shape[1] - 1).bit_length())
            for bit in range(nbits - 1, -1, -1):
                cand = p + (1 << bit)
                f = count(lambda x, idx: jnp.logical_and(jnp.logical_and(x >= lo, x < hi), idx < cand))
                p = jnp.where(f < r, cand, p)

            def drop(g, carry):
                c0 = pl.multiple_of(g * LANES, LANES)
                x = isc_scr[:, pl.ds(c0, LANES)]
                ex = jnp.logical_and(jnp.logical_and(tie, (c0 + lane) > p),
                                     jnp.logical_and(x >= lo, x < hi))
                isc_scr[:, pl.ds(c0, LANES)] = jnp.where(ex, -jnp.inf, x)
                return carry

            lax.fori_loop(0, ng, drop, 0)

    thr = t_scr[...]
    for h in range(B_HEADS):
        lhs_scr[h * TQ:(h + 1) * TQ, 0:KV_RANK] = qlat_ref[:, h * KV_RANK:(h + 1) * KV_RANK]

    @pl.when(qi == 0)
    def _():
        eye = jnp.where(lax.broadcasted_iota(I32, (TQ, TQ), 0) == lax.broadcasted_iota(I32, (TQ, TQ), 1),
                        1.0, 0.0).astype(BF16)
        for h in range(B_HEADS):
            lhs_scr[h * TQ:(h + 1) * TQ, KV_RANK:] = eye

    m_scr[...] = jnp.full(m_scr.shape, NEG, F32)
    acc_scr[...] = jnp.zeros_like(acc_scr)

    def lanes_x(v, width):
        return jnp.concatenate([v] * (width // LANES), axis=1)

    def attend(k0, width, key_lim, with_bias):
        x = isc_scr[:, pl.ds(k0, width)]
        kidx = k0 + lax.broadcasted_iota(I32, (TQ, width), 1)
        sel = jnp.logical_and(x >= lanes_x(thr, width), kidx < key_lim)
        rhs_scr[0:KV_RANK, 0:width] = ckvt_ref[:, pl.ds(k0, width)]
        rhs_scr[KV_RANK:, 0:width] = jnp.where(sel, 0.0, NEG).astype(BF16)
        rhs = rhs_scr[:, 0:width]
        kv = ckvx_ref[pl.ds(k0, width), :]
        for h in range(B_HEADS):
            rs = slice(h * TQ, (h + 1) * TQ)
            s = _dot(lhs_scr[rs, :], rhs)
            if with_bias:
                s = s + biasd_ref[0, h]
            m_old = m_scr[rs, :]
            m_new = jnp.maximum(m_old, jnp.max(s, axis=1, keepdims=True))
            alpha = jnp.exp2(m_old - m_new)
            p = jnp.exp2(s - lanes_x(m_new, width))
            acc_scr[rs, :] = acc_scr[rs, :] * lanes_x(alpha, 2 * KV_RANK) + _dot(p.astype(BF16), kv)
            m_scr[rs, :] = m_new

    far_lim = jnp.maximum(q0 - TQ, 0)
    nf = (far_lim + TK - 1) // TK

    def far_block(jb, carry):
        attend(pl.multiple_of(jb * TK, TK), TK, far_lim, False)
        return carry

    lax.fori_loop(0, nf, far_block, 0)
    attend(pl.multiple_of(far_lim, TQ), TKN, lanes_x(lim, TKN), True)

    for h in range(B_HEADS):
        rs = slice(h * TQ, (h + 1) * TQ)
        a = acc_scr[rs, :]
        olat = (a[:, 0:KV_RANK] / a[:, KV_RANK:]).astype(BF16)
        o_ref[:, h * B_V_DIM:(h + 1) * B_V_DIM] = _dot(olat, wuv_ref[h]).astype(o_ref.dtype)


def _dsa(qlat, qidx, widx, kidxt, ckvt, ckvx, biasd, w_uv, top_k):
    s = qlat.shape[0]
    assert s % TKA == 0 and top_k >= CHUNK
    return pl.pallas_call(
        functools.partial(_dsa_kernel, top_k=top_k),
        out_shape=jax.ShapeDtypeStruct((s, B_WIDTH), BF16),
        grid=(s // TQ,),
        in_specs=[
            pl.BlockSpec((TQ, B_HEADS * KV_RANK), lambda i: (i, 0)),
            pl.BlockSpec((TQ, IDX_HEADS * IDX_DIM), lambda i: (i, 0)),
            pl.BlockSpec((TQ, IDX_HEADS), lambda i: (i, 0)),
            _const_spec((IDX_DIM, s)),
            _const_spec((KV_RANK, s)),
            _const_spec((s, 2 * KV_RANK)),
            pl.BlockSpec((1, B_HEADS, TQ, TKN), lambda i: (jnp.minimum(i, 1), 0, 0, 0)),
            _const_spec((B_HEADS, KV_RANK, B_V_DIM)),
        ],
        out_specs=pl.BlockSpec((TQ, B_WIDTH), lambda i: (i, 0)),
        scratch_shapes=[
            pltpu.VMEM((TQ, s), F32),
            pltpu.VMEM((TQ, LANES), F32),
            pltpu.VMEM((B_HEADS * TQ, 2 * KV_RANK), BF16),
            pltpu.VMEM((2 * KV_RANK, TK), BF16),
            pltpu.VMEM((B_HEADS * TQ, LANES), F32),
            pltpu.VMEM((B_HEADS * TQ, 2 * KV_RANK), F32),
        ],
        compiler_params=_cparams(("arbitrary",)),
        name="dsa",
    )(qlat, qidx, widx, kidxt, ckvt, ckvx, biasd, w_uv)


def _merge_kernel(x_ref, hg_ref, ob_ref, ga_ref, gb_ref, wa_ref, wb_ref, wo_ref, fn_ref,
                  x1_ref, h2_ref):
    ya = _dot(hg_ref[...], wa_ref[...])
    yb = _dot(ob_ref[...], wb_ref[...])
    mix = _sigmoid(ga_ref[...]) * ya + _sigmoid(gb_ref[...]) * yb
    x1 = x_ref[...] + _dot(mix.astype(BF16), wo_ref[...])
    x1_ref[...] = x1
    h2 = x1 * lax.rsqrt(jnp.mean(x1 * x1, axis=-1, keepdims=True) + EPS) * fn_ref[...]
    h2_ref[...] = h2.astype(BF16)


def _merge(x2, hg, ob, proj, w_a, w_b, w_o, ffn_norm, tm):
    s, d = x2.shape
    gblk = COL_GATE // d
    row = lambda i: (i, 0)
    return pl.pallas_call(
        _merge_kernel,
        out_shape=[jax.ShapeDtypeStruct((s, d), F32), jax.ShapeDtypeStruct((s, d), BF16)],
        grid=(s // tm,),
        in_specs=[
            pl.BlockSpec((tm, d), row),
            pl.BlockSpec((tm, A_WIDTH), row),
            pl.BlockSpec((tm, B_WIDTH), row),
            pl.BlockSpec((tm, d), lambda i: (i, gblk)),
            pl.BlockSpec((tm, d), lambda i: (i, gblk + 1)),
            _const_spec((A_WIDTH, d)),
            _const_spec((B_WIDTH, d)),
            _const_spec((d, d)),
            _const_spec((1, d)),
        ],
        out_specs=[pl.BlockSpec((tm, d), row), pl.BlockSpec((tm, d), row)],
        compiler_params=_cparams(("arbitrary",)),
        name="merge",
    )(x2, hg, ob, proj, proj, w_a, w_b, w_o, ffn_norm)


def _ffn_kernel(x1_ref, h2_ref, wup_ref, cw_ref, cb_ref, wdn_ref, fin_ref, o_ref,
                tail_scr, u_scr, act_scr, *, cbw):
    tm = h2_ref.shape[0]
    dff = wdn_ref.shape[0]
    halo = 8

    @pl.when(pl.program_id(0) == 0)
    def _():
        tail_scr[...] = jnp.zeros_like(tail_scr)

    h2 = h2_ref[...]

    def conv_cols(c0):
        u = _dot(h2, wup_ref[:, c0:c0 + cbw])
        u_scr[0:halo, :] = tail_scr[:, c0:c0 + cbw]
        u_scr[halo:, :] = u
        tail_scr[:, c0:c0 + cbw] = u[tm - halo:, :]
        cw = cw_ref[:, c0:c0 + cbw]
        return (cw[2:3] * u + cw[1:2] * u_scr[halo - 1:halo - 1 + tm, :]
                + cw[0:1] * u_scr[halo - 2:halo - 2 + tm, :] + cb_ref[:, c0:c0 + cbw])

    for j in range(dff // cbw):
        a = conv_cols(j * cbw)
        v = conv_cols(dff + j * cbw)
        act_scr[:, j * cbw:(j + 1) * cbw] = (a * _sigmoid(a) * v).astype(BF16)

    y = x1_ref[...] + _dot(act_scr[...], wdn_ref[...])
    o_ref[...] = y * lax.rsqrt(jnp.mean(y * y, axis=-1, keepdims=True) + EPS) * fin_ref[...]


def _ffn(x1, h2, w_up, conv_w, conv_b, w_down, final_norm, tm, cbw):
    s, d = x1.shape
    dff = w_down.shape[0]
    row = lambda i: (i, 0)
    return pl.pallas_call(
        functools.partial(_ffn_kernel, cbw=cbw),
        out_shape=jax.ShapeDtypeStruct((s, d), F32),
        grid=(s // tm,),
        in_specs=[
            pl.BlockSpec((tm, d), row),
            pl.BlockSpec((tm, d), row),
            _const_spec((d, 2 * dff)),
            _const_spec((CONV_WIDTH, 2 * dff)),
            _const_spec((1, 2 * dff)),
            _const_spec((dff, d)),
            _const_spec((1, d)),
        ],
        out_specs=pl.BlockSpec((tm, d), row),
        scratch_shapes=[
            pltpu.VMEM((8, 2 * dff), F32),
            pltpu.VMEM((tm + 8, cbw), F32),
            pltpu.VMEM((tm, dff), BF16),
        ],
        compiler_params=_cparams(("arbitrary",)),
        name="ffn",
    )(x1, h2, w_up, conv_w, conv_b, w_down, final_norm)


def _row_tile(s, want):
    t = min(want, s)
    while s % t:
        t //= 2
    return t


def kernel(x, attn_norm, w_in, lower_bounds, hgrn_out_norm, q_norm, kv_norm, w_uq, w_uk, w_uv, w_iq,
           rel_bias, w_branch_a, w_branch_b, w_o, ffn_norm, w_up, conv_w, conv_b, w_down, final_norm):
    bsz, s, d = x.shape
    assert bsz == 1 and attn_norm.shape[0] == 1 and lower_bounds.shape[0] == 2
    x2 = x.reshape(s, d)
    top_k = min(TOPK_MAX, s // 4)

    wi = w_in[0]
    n_small = Q_RANK + KV_RANK + IDX_DIM + IDX_HEADS
    w_pad = jnp.concatenate(
        [wi[:, :COL_GATE], wi[:, COL_GATE + n_small:], wi[:, COL_GATE:COL_GATE + n_small],
         jnp.zeros((d, SMALL_W - n_small), wi.dtype)], axis=1).astype(BF16)

    proj = _inproj(x2, attn_norm, w_pad, _row_tile(s, 1024), N_PROJ // 4)

    hg = _hgrn(proj, lower_bounds, hgrn_out_norm, _row_tile(s, 512))

    qlat, qidx, widx, ckvx, ckvt, kidxt = _mla_prep(
        proj, q_norm, kv_norm, w_uq[0].astype(BF16), jnp.swapaxes(w_uk[0], 1, 2).astype(BF16),
        w_iq[0].astype(BF16), _row_tile(s, 512))

    t_loc = jnp.arange(TQ, dtype=jnp.int32)[None, :, None]
    s_loc = jnp.arange(TKN, dtype=jnp.int32)[None, None, :]
    off = jnp.array([0, -TQ], jnp.int32)[:, None, None]
    buckets = _t5_bucket(s_loc - t_loc + off)
    biasd = _bias_tab(buckets, rel_bias)

    ob = _dsa(qlat, qidx, widx, kidxt, ckvt, ckvx, biasd, w_uv[0].astype(BF16), top_k)

    x1, h2 = _merge(x2, hg, ob, proj, w_branch_a[0].astype(BF16), w_branch_b[0].astype(BF16),
                    w_o[0].astype(BF16), ffn_norm, _row_tile(s, 512))

    out = _ffn(x1, h2, w_up[0].astype(BF16), conv_w[0], conv_b, w_down[0].astype(BF16),
               final_norm.reshape(1, d), _row_tile(s, 512), 256)
    return out.reshape(bsz, s, d)
```

```python
import functools
import math

import jax
import jax.numpy as jnp
from jax import lax
from jax.experimental import pallas as pl
from jax.experimental.pallas import tpu as pltpu

F32 = jnp.float32
BF16 = jnp.bfloat16
I32 = jnp.int32

EPS = 1e-6
CHUNK = 64
A_HEADS = 8
A_HEAD_DIM = 128
A_WIDTH = A_HEADS * A_HEAD_DIM
B_HEADS = 16
B_QK_DIM = 64
B_V_DIM = 64
B_WIDTH = B_HEADS * B_V_DIM
Q_RANK = 256
KV_RANK = 128
IDX_HEADS = 8
IDX_DIM = 64
TOPK_MAX = 256
N_BUCKETS = 32
MAX_DISTANCE = 128
CONV_WIDTH = 3

LANES = 128
VMEM_LIMIT = 56 * 1024 * 1024

COL_QFIG = 0
COL_GATE = 4 * A_WIDTH
COL_SMALL = COL_GATE + 2048
SMALL_W = 512
N_PROJ = COL_SMALL + SMALL_W

TQ = 128
TKA = 512
TK = 512
TKN = 2 * TQ
NEG = -1e30
EXP_SAFE = 80.0
MAX_BISECT = 1024
LOG2E = 1.4426950408889634


def _cparams(sem):
    return pltpu.CompilerParams(dimension_semantics=sem, vmem_limit_bytes=VMEM_LIMIT)


def _const_spec(shape):
    nd = len(shape)
    return pl.BlockSpec(shape, lambda *_: (0,) * nd, pipeline_mode=pl.Buffered(1))


def _dot(a, b):
    return jnp.dot(a, b, preferred_element_type=F32)


def _dot_nt(a, b):
    return lax.dot_general(a, b, (((1,), (1,)), ((), ())), preferred_element_type=F32)


def _dot_tn(a, b):
    return lax.dot_general(a, b, (((0,), (0,)), ((), ())), preferred_element_type=F32)


def _sigmoid(x):
    return jax.nn.sigmoid(x)


def _inproj_kernel(x_ref, g_ref, w_ref, o_ref, h_scr):
    @pl.when(pl.program_id(1) == 0)
    def _():
        x = x_ref[...]
        y = x * lax.rsqrt(jnp.mean(x * x, axis=-1, keepdims=True) + EPS) * g_ref[...]
        h_scr[...] = y.astype(BF16)

    o_ref[...] = _dot(h_scr[...], w_ref[...])


def _inproj(x2, gain, w_pad, tm, tn):
    s, d = x2.shape
    n = w_pad.shape[1]
    return pl.pallas_call(
        _inproj_kernel,
        out_shape=jax.ShapeDtypeStruct((s, n), F32),
        grid=(s // tm, n // tn),
        in_specs=[
            pl.BlockSpec((tm, d), lambda i, j: (i, 0)),
            pl.BlockSpec((1, d), lambda i, j: (0, 0)),
            pl.BlockSpec((d, tn), lambda i, j: (0, j)),
        ],
        out_specs=pl.BlockSpec((tm, tn), lambda i, j: (i, j)),
        scratch_shapes=[pltpu.VMEM((tm, d), BF16)],
        compiler_params=_cparams(("arbitrary", "arbitrary")),
        name="inproj",
    )(x2, gain, w_pad)


def _hgrn_kernel(q_ref, f_ref, i_ref, g_ref, lbp_ref, on_ref, o_ref,
                 st_ref, b_scr, k_scr, qs_scr, att_scr, *, nch):
    c_ = CHUNK
    half = c_ // 2

    @pl.when(pl.program_id(1) == 0)
    def _():
        st_ref[...] = jnp.zeros_like(st_ref)

    l0 = lbp_ref[0:1, :]
    l1 = lbp_ref[1:2, :]
    lm = jnp.maximum(l0, l1)
    e0 = jnp.exp(l0 - lm)
    e1 = jnp.exp(l1 - lm)
    lb = e0 / (e0 + e1)

    row = lax.broadcasted_iota(I32, (c_, c_), 0)
    col = lax.broadcasted_iota(I32, (c_, c_), 1)
    tri = row >= col
    tri_bf = jnp.where(tri, 1.0, 0.0).astype(BF16)

    q = q_ref[...]
    qs_scr[...] = q * _sigmoid(q)
    fg = lb + (1.0 - lb) * _sigmoid(f_ref[...])
    k_scr[...] = 1.0 - fg
    lg = jnp.log(fg)

    dev = jnp.zeros((1, 1), F32)
    for c in range(nch):
        l = lg[c * c_:(c + 1) * c_]
        hi = l.astype(BF16)
        r1 = l - hi.astype(F32)
        mid = r1.astype(BF16)
        lo = (r1 - mid.astype(F32)).astype(BF16)
        b = _dot(tri_bf, hi) + _dot(tri_bf, mid) + _dot(tri_bf, lo)
        b_scr[c * c_:(c + 1) * c_, :] = b
        rho = b[half - 1:half, :]
        dev = jnp.maximum(dev, jnp.max(jnp.abs(b - rho), keepdims=True))
    safe = dev[0, 0] <= EXP_SAFE

    @pl.when(safe)
    def _():
        for c in range(nch):
            sl = slice(c * c_, (c + 1) * c_)
            b = b_scr[sl, :]
            rho = b[half - 1:half, :]
            qt = (qs_scr[sl, :] * jnp.exp(b - rho)).astype(BF16)
            kt = (k_scr[sl, :] * jnp.exp(rho - b)).astype(BF16)
            att_scr[c] = jnp.where(tri, _dot_nt(qt, kt), 0.0)

    @pl.when(jnp.logical_not(safe))
    def _():
        trow = lax.broadcasted_iota(I32, (c_, A_HEAD_DIM), 0)
        for c in range(nch):
            sl = slice(c * c_, (c + 1) * c_)
            b = b_scr[sl, :]
            qv = qs_scr[sl, :]

            def body(s, att):
                bs = b_scr[pl.ds(c * c_ + s, 1), :]
                ks = k_scr[pl.ds(c * c_ + s, 1), :]
                d = jnp.where(trow >= s, b - bs, -jnp.inf)
                colv = jnp.sum(qv * ks * jnp.exp(d), axis=-1, keepdims=True)
                return jnp.where(col == s, colv, att)

            att_scr[c] = lax.fori_loop(0, c_, body, jnp.zeros((c_, c_), F32))

    on = on_ref[...]
    for c in range(nch):
        sl = slice(c * c_, (c + 1) * c_)
        b = b_scr[sl, :]
        bend = b[c_ - 1:c_, :]
        v = i_ref[sl, :].astype(BF16)
        st = st_ref[...]
        qe = (qs_scr[sl, :] * jnp.exp(b)).astype(BF16)
        o = _dot_nt(qe, st.astype(BF16)) + _dot(att_scr[c].astype(BF16), v)
        kd = (k_scr[sl, :] * jnp.exp(bend - b)).astype(BF16)
        st_ref[...] = st * jnp.exp(bend) + _dot_tn(v, kd)
        o = o * lax.rsqrt(jnp.mean(o * o, axis=-1, keepdims=True) + EPS) * on
        g = g_ref[sl, :]
        o_ref[sl, :] = (o * (g * _sigmoid(g))).astype(o_ref.dtype)


def _hgrn(proj, lower_bounds, out_norm, t_rows):
    s = proj.shape[0]
    nch = t_rows // CHUNK
    d = A_HEAD_DIM
    nb = COL_QFIG // d

    def colspec(off):
        return pl.BlockSpec((t_rows, d), lambda h, i, off=off: (i, nb + off * A_HEADS + h))

    return pl.pallas_call(
        functools.partial(_hgrn_kernel, nch=nch),
        out_shape=jax.ShapeDtypeStruct((s, A_WIDTH), BF16),
        grid=(A_HEADS, s // t_rows),
        in_specs=[
            colspec(0), colspec(1), colspec(2), colspec(3),
            pl.BlockSpec((2, d), lambda h, i: (0, h)),
            pl.BlockSpec((1, d), lambda h, i: (0, 0)),
        ],
        out_specs=pl.BlockSpec((t_rows, d), lambda h, i: (i, h)),
        scratch_shapes=[
            pltpu.VMEM((d, d), F32),
            pltpu.VMEM((t_rows, d), F32),
            pltpu.VMEM((t_rows, d), F32),
            pltpu.VMEM((t_rows, d), F32),
            pltpu.VMEM((nch, CHUNK, CHUNK), F32),
        ],
        compiler_params=_cparams(("arbitrary", "arbitrary")),
        name="hgrn",
    )(proj, proj, proj, proj, lower_bounds, out_norm)


def _mla_prep_kernel(p_ref, qn_ref, kn_ref, wuq_ref, wukt_ref, wiq_ref,
                     qlat_ref, qidx_ref, widx_ref, ckvx_ref, ckvt_ref, kidxt_ref):
    p = p_ref[...]
    cq = p[:, 0:Q_RANK]
    cq = cq * lax.rsqrt(jnp.mean(cq * cq, axis=-1, keepdims=True) + EPS) * qn_ref[...]
    cqb = cq.astype(BF16)
    ckv = p[:, Q_RANK:Q_RANK + KV_RANK]
    ckv = ckv * lax.rsqrt(jnp.mean(ckv * ckv, axis=-1, keepdims=True) + EPS) * kn_ref[...]

    q = _dot(cqb, wuq_ref[...])
    scale = B_QK_DIM ** -0.5 * LOG2E
    for h in range(B_HEADS):
        qh = q[:, h * B_QK_DIM:(h + 1) * B_QK_DIM].astype(BF16)
        qlat_ref[:, h * KV_RANK:(h + 1) * KV_RANK] = (_dot(qh, wukt_ref[h]) * scale).astype(BF16)
    qidx_ref[...] = _dot(cqb, wiq_ref[...]).astype(BF16)

    rest = p[:, Q_RANK + KV_RANK:]
    widx_ref[...] = rest[:, IDX_DIM:IDX_DIM + IDX_HEADS] * ((IDX_HEADS * IDX_DIM) ** -0.5)
    ckvx_ref[:, 0:KV_RANK] = ckv.astype(BF16)
    ckvx_ref[:, KV_RANK:] = jnp.ones((p.shape[0], KV_RANK), BF16)
    ckvt_ref[...] = ckv.T.astype(BF16)
    kidxt_ref[...] = rest.T[0:IDX_DIM, :].astype(BF16)


def _mla_prep(proj, q_norm, kv_norm, w_uq, w_ukt, w_iq, tm):
    s = proj.shape[0]
    return pl.pallas_call(
        _mla_prep_kernel,
        out_shape=[
            jax.ShapeDtypeStruct((s, B_HEADS * KV_RANK), BF16),
            jax.ShapeDtypeStruct((s, IDX_HEADS * IDX_DIM), BF16),
            jax.ShapeDtypeStruct((s, IDX_HEADS), F32),
            jax.ShapeDtypeStruct((s, 2 * KV_RANK), BF16),
            jax.ShapeDtypeStruct((KV_RANK, s), BF16),
            jax.ShapeDtypeStruct((IDX_DIM, s), BF16),
        ],
        grid=(s // tm,),
        in_specs=[
            pl.BlockSpec((tm, SMALL_W), lambda i: (i, COL_SMALL // SMALL_W)),
            _const_spec((1, Q_RANK)),
            _const_spec((1, KV_RANK)),
            _const_spec((Q_RANK, B_HEADS * B_QK_DIM)),
            _const_spec((B_HEADS, B_QK_DIM, KV_RANK)),
            _const_spec((Q_RANK, IDX_HEADS * IDX_DIM)),
        ],
        out_specs=[
            pl.BlockSpec((tm, B_HEADS * KV_RANK), lambda i: (i, 0)),
            pl.BlockSpec((tm, IDX_HEADS * IDX_DIM), lambda i: (i, 0)),
            pl.BlockSpec((tm, IDX_HEADS), lambda i: (i, 0)),
            pl.BlockSpec((tm, 2 * KV_RANK), lambda i: (i, 0)),
            pl.BlockSpec((KV_RANK, tm), lambda i: (0, i)),
            pl.BlockSpec((IDX_DIM, tm), lambda i: (0, i)),
        ],
        compiler_params=_cparams(("arbitrary",)),
        name="mla_prep",
    )(proj, q_norm, kv_norm, w_uq, w_ukt, w_iq)


def _bias_tab_kernel(bkt_ref, rb_ref, o_ref):
    far = N_BUCKETS // 2 - 1
    for v in range(2):
        bkt = bkt_ref[v]
        for h in range(B_HEADS):
            acc = jnp.zeros(bkt.shape, F32)
            for k in range(N_BUCKETS):
                acc = jnp.where(bkt == k, (rb_ref[k, h] - rb_ref[far, h]) * LOG2E, acc)
            o_ref[v, h] = acc


def _bias_tab(buckets, rel_bias):
    return pl.pallas_call(
        _bias_tab_kernel,
        out_shape=jax.ShapeDtypeStruct((2, B_HEADS, TQ, TKN), F32),
        in_specs=[
            pl.BlockSpec(memory_space=pltpu.VMEM),
            pl.BlockSpec(memory_space=pltpu.SMEM),
        ],
        out_specs=pl.BlockSpec(memory_space=pltpu.VMEM),
        compiler_params=pltpu.CompilerParams(vmem_limit_bytes=VMEM_LIMIT),
        name="bias_tab",
    )(buckets, rel_bias)


def _t5_bucket(rel):
    half = N_BUCKETS // 2
    max_exact = half // 2
    ret = jnp.where(rel > 0, half, 0)
    n = jnp.abs(rel)
    nf = jnp.maximum(n, 1).astype(jnp.float32)
    large = max_exact + (jnp.log(nf / max_exact) / math.log(MAX_DISTANCE / max_exact) * (half - max_exact)).astype(jnp.int32)
    large = jnp.minimum(large, half - 1)
    return ret + jnp.where(n < max_exact, n, large)


def _dsa_kernel(qlat_ref, qidx_ref, widx_ref, kidxt_ref, ckvt_ref, ckvx_ref, biasd_ref, wuv_ref,
                o_ref, isc_scr, t_scr, lhs_scr, rhs_scr, m_scr, acc_scr, p_scr, al_scr, qh_scr, wb_scr,
                *, top_k):
    qi = pl.program_id(0)
    q0 = qi * TQ
    kf = float(top_k)
    lane = lax.broadcasted_iota(I32, (TQ, LANES), 1)
    rowi = lax.broadcasted_iota(I32, (TQ, LANES), 0)
    lim = q0 + (rowi // CHUNK + 1) * CHUNK

    na = (q0 + TQ + TKA - 1) // TKA
    qidx = qidx_ref[...]
    w = widx_ref[...]
    for h in range(IDX_HEADS):
        qh_scr[h] = qidx[:, h * IDX_DIM:(h + 1) * IDX_DIM]
        wb_scr[h] = jnp.broadcast_to(w[:, h:h + 1], (TQ, LANES))

    def lanes_x(v, width):
        return jnp.concatenate([v] * (width // LANES), axis=1)

    def scores(k0):
        kt = kidxt_ref[:, pl.ds(k0, TKA)]
        acc = jnp.zeros((TQ, TKA), F32)
        for h in range(IDX_HEADS):
            acc = acc + jnp.maximum(_dot(qh_scr[h], kt), 0.0) * lanes_x(wb_scr[h], TKA)
        return acc

    def emit(k0, acc, mn, mx):
        for g in range(TKA // LANES):
            a = acc[:, g * LANES:(g + 1) * LANES]
            mn = jnp.minimum(mn, a)
            mx = jnp.maximum(mx, a)
            adm = (k0 + g * LANES + lane) < lim
            isc_scr[:, pl.ds(k0 + g * LANES, LANES)] = jnp.where(adm, a, -jnp.inf)
        return mn, mx

    def phase_a(jb, carry):
        mn, mx = carry
        k0 = pl.multiple_of(jb * (2 * TKA), 2 * TKA)
        a0 = scores(k0)
        a1 = scores(k0 + TKA)
        mn, mx = emit(k0, a0, mn, mx)
        return emit(k0 + TKA, a1, mn, mx)

    mn, mx = lax.fori_loop(0, (na + 1) // 2, phase_a,
                           (jnp.full((TQ, LANES), jnp.inf, F32), jnp.full((TQ, LANES), -jnp.inf, F32)))
    ng = na * (TKA // LANES)

    def count(pred):
        def body(g, cnt):
            c0 = pl.multiple_of(g * LANES, LANES)
            x = isc_scr[:, pl.ds(c0, LANES)]
            return cnt + jnp.where(pred(x, c0 + lane), 1.0, 0.0)
        cnt = lax.fori_loop(0, ng, body, jnp.zeros((TQ, LANES), F32))
        return jnp.broadcast_to(jnp.sum(cnt, axis=1, keepdims=True), (TQ, LANES))

    t_scr[...] = jnp.full((TQ, LANES), -jnp.inf, F32)

    @pl.when(q0 + CHUNK > top_k)
    def _():
        def row_total(part):
            return jnp.broadcast_to(jnp.sum(part, axis=1, keepdims=True), (TQ, LANES))

        rmin = jnp.broadcast_to(jnp.min(mn, axis=1, keepdims=True), (TQ, LANES))
        rmax = jnp.broadcast_to(jnp.max(mx, axis=1, keepdims=True), (TQ, LANES))
        n_adm = lim.astype(F32)
        def count_zero():
            def body(g, cs):
                c0 = pl.multiple_of(g * LANES, LANES)
                x = isc_scr[:, pl.ds(c0, LANES)]
                return cs[0] + jnp.where(x >= 0.0, 1.0, 0.0), cs[1] + jnp.where(x > 0.0, 1.0, 0.0)
            z = jnp.zeros((TQ, LANES), F32)
            ca, cb = lax.fori_loop(0, ng, body, (z, z))
            return row_total(ca), row_total(cb)

        c_ge0, c_gt0 = count_zero()
        above = c_gt0 >= kf
        below = c_ge0 < kf
        tiny = float(jnp.finfo(F32).tiny)
        hi_top = rmax + (rmax - rmin) + jnp.abs(rmax) * 1e-3 + 1e-30
        lo0 = jnp.where(below, rmin, 0.0)
        clo0 = jnp.where(below, n_adm, c_ge0)
        hi0 = jnp.where(below, 0.0, jnp.where(above, hi_top, tiny))
        chi0 = jnp.where(below, c_ge0, jnp.where(above, 0.0, c_gt0))
        log_k = math.log(kf)

        def midpoint(lo, hi):
            return 0.5 * lo + 0.5 * hi

        def is_open(lo, hi):
            mid = midpoint(lo, hi)
            return jnp.logical_and(mid > lo, mid < hi)

        def n_active(lo, hi, done):
            act = jnp.logical_and(done == 0.0, is_open(lo, hi))
            return jnp.sum(jnp.where(act, 1.0, 0.0))

        def pivot(it, lo, hi, clo, chi):
            llo = jnp.log(clo)
            lhi = jnp.log(jnp.maximum(chi, 0.5))
            t = jnp.clip((llo - log_k) / (llo - lhi), 0.03, 0.97)
            guess = lo + t * (hi - lo)
            mid = midpoint(lo, hi)
            use = jnp.logical_and(jnp.logical_and(guess > lo, guess < hi), it % 3 != 2)
            return jnp.where(use, guess, mid)

        def count_ge(mid):
            w = 4 * LANES
            mid_w = lanes_x(mid, w)

            def body(g, cnt):
                c0 = pl.multiple_of(g * w, w)
                ind = jnp.where(isc_scr[:, pl.ds(c0, w)] >= mid_w, 1.0, 0.0)
                return cnt + ((ind[:, 0:LANES] + ind[:, LANES:2 * LANES])
                              + (ind[:, 2 * LANES:3 * LANES] + ind[:, 3 * LANES:]))

            return row_total(lax.fori_loop(0, ng // 4, body, jnp.zeros((TQ, LANES), F32)))

        def cond(st):
            it, nact = st[0], st[1]
            return jnp.logical_and(it < MAX_BISECT, nact > 0.0)

        def body(st):
            it, nact, lo, hi, clo, chi, done = st
            act = jnp.logical_and(done == 0.0, is_open(lo, hi))
            mid = pivot(it, lo, hi, clo, chi)
            c = count_ge(mid)
            up = jnp.logical_and(act, c >= kf)
            dn = jnp.logical_and(act, c < kf)
            lo = jnp.where(up, mid, lo)
            clo = jnp.where(up, c, clo)
            hi = jnp.where(dn, mid, hi)
            chi = jnp.where(dn, c, chi)
            done = jnp.where(jnp.logical_and(act, c == kf), 1.0, done)
            return it + 1, n_active(lo, hi, done), lo, hi, clo, chi, done

        done0 = jnp.where(c_ge0 == kf, 1.0, 0.0)
        _, _, t_lo, t_hi, _, _, done = lax.while_loop(
            cond, body, (jnp.int32(0), n_active(lo0, hi0, done0), lo0, hi0, clo0, chi0, done0))
        t_scr[...] = t_lo

        n_tie = jnp.sum(jnp.where(done == 0.0, 1.0, 0.0))

        @pl.when(n_tie > 0.0)
        def _():
            lo, hi, tie = t_lo, t_hi, done == 0.0
            r = kf - count(lambda x, idx: x >= hi)
            p = jnp.zeros((TQ, LANES), I32)
            nbits = max(1, (isc_scr.shape[1] - 1).bit_length())
            for bit in range(nbits - 1, -1, -1):
                cand = p + (1 << bit)
                f = count(lambda x, idx: jnp.logical_and(jnp.logical_and(x >= lo, x < hi), idx < cand))
                p = jnp.where(f < r, cand, p)

            def drop(g, carry):
                c0 = pl.multiple_of(g * LANES, LANES)
                x = isc_scr[:, pl.ds(c0, LANES)]
                ex = jnp.logical_and(jnp.logical_and(tie, (c0 + lane) > p),
                                     jnp.logical_and(x >= lo, x < hi))
                isc_scr[:, pl.ds(c0, LANES)] = jnp.where(ex, -jnp.inf, x)
                return carry

            lax.fori_loop(0, ng, drop, 0)

    thr = t_scr[...]
    for h in range(B_HEADS):
        lhs_scr[h * TQ:(h + 1) * TQ, 0:KV_RANK] = qlat_ref[:, h * KV_RANK:(h + 1) * KV_RANK]

    @pl.when(qi == 0)
    def _():
        eye = jnp.where(lax.broadcasted_iota(I32, (TQ, TQ), 0) == lax.broadcasted_iota(I32, (TQ, TQ), 1),
                        1.0, 0.0).astype(BF16)
        for h in range(B_HEADS):
            lhs_scr[h * TQ:(h + 1) * TQ, KV_RANK:] = eye

    m_scr[...] = jnp.full(m_scr.shape, NEG, F32)
    acc_scr[...] = jnp.zeros_like(acc_scr)

    def accumulate(h, k0_prev, w_prev):
        rs = slice(h * TQ, (h + 1) * TQ)
        kv = ckvx_ref[pl.ds(k0_prev, w_prev), :]
        acc_scr[rs, :] = (acc_scr[rs, :] * lanes_x(al_scr[rs, :], 2 * KV_RANK)
                          + _dot(p_scr[rs, 0:w_prev], kv))

    def stage(k0_prev, w_prev, k0, width, key_lim, with_bias):
        x = isc_scr[:, pl.ds(k0, width)]
        kidx = k0 + lax.broadcasted_iota(I32, (TQ, width), 1)
        sel = jnp.logical_and(x >= lanes_x(thr, width), kidx < key_lim)
        rhs_scr[0:KV_RANK, 0:width] = ckvt_ref[:, pl.ds(k0, width)]
        rhs_scr[KV_RANK:, 0:width] = jnp.where(sel, 0.0, NEG).astype(BF16)
        rhs = rhs_scr[:, 0:width]
        for h in range(B_HEADS):
            rs = slice(h * TQ, (h + 1) * TQ)
            if w_prev:
                accumulate(h, k0_prev, w_prev)
            s = _dot(lhs_scr[rs, :], rhs)
            if with_bias:
                s = s + biasd_ref[0, h]
            m_old = m_scr[rs, :]
            m_new = jnp.maximum(m_old, jnp.max(s, axis=1, keepdims=True))
            al_scr[rs, :] = jnp.exp2(m_old - m_new)
            p_scr[rs, 0:width] = jnp.exp2(s - lanes_x(m_new, width)).astype(BF16)
            m_scr[rs, :] = m_new

    far_lim = jnp.maximum(q0 - TQ, 0)
    nf = (far_lim + TK - 1) // TK

    @pl.when(nf == 0)
    def _():
        p_scr[...] = jnp.zeros_like(p_scr)
        al_scr[...] = jnp.ones_like(al_scr)

    @pl.when(nf > 0)
    def _():
        stage(0, 0, 0, TK, far_lim, False)

    def far_block(jb, carry):
        stage(pl.multiple_of((jb - 1) * TK, TK), TK, pl.multiple_of(jb * TK, TK), TK, far_lim, False)
        return carry

    lax.fori_loop(1, nf, far_block, 0)
    k0_last = pl.multiple_of(jnp.maximum(nf - 1, 0) * TK, TK)
    k0_near = pl.multiple_of(far_lim, TQ)
    stage(k0_last, TK, k0_near, TKN, lanes_x(lim, TKN), True)
    for h in range(B_HEADS):
        accumulate(h, k0_near, TKN)

    for h in range(B_HEADS):
        rs = slice(h * TQ, (h + 1) * TQ)
        a = acc_scr[rs, :]
        olat = (a[:, 0:KV_RANK] / a[:, KV_RANK:]).astype(BF16)
        o_ref[:, h * B_V_DIM:(h + 1) * B_V_DIM] = _dot(olat, wuv_ref[h]).astype(o_ref.dtype)


def _dsa(qlat, qidx, widx, kidxt, ckvt, ckvx, biasd, w_uv, top_k):
    s = qlat.shape[0]
    assert s % (2 * TKA) == 0 and TK == TKA and top_k % TQ == 0
    return pl.pallas_call(
        functools.partial(_dsa_kernel, top_k=top_k),
        out_shape=jax.ShapeDtypeStruct((s, B_WIDTH), BF16),
        grid=(s // TQ,),
        in_specs=[
            pl.BlockSpec((TQ, B_HEADS * KV_RANK), lambda i: (i, 0)),
            pl.BlockSpec((TQ, IDX_HEADS * IDX_DIM), lambda i: (i, 0)),
            pl.BlockSpec((TQ, IDX_HEADS), lambda i: (i, 0)),
            _const_spec((IDX_DIM, s)),
            _const_spec((KV_RANK, s)),
            _const_spec((s, 2 * KV_RANK)),
            pl.BlockSpec((1, B_HEADS, TQ, TKN), lambda i: (jnp.minimum(i, 1), 0, 0, 0)),
            _const_spec((B_HEADS, KV_RANK, B_V_DIM)),
        ],
        out_specs=pl.BlockSpec((TQ, B_WIDTH), lambda i: (i, 0)),
        scratch_shapes=[
            pltpu.VMEM((TQ, s), F32),
            pltpu.VMEM((TQ, LANES), F32),
            pltpu.VMEM((B_HEADS * TQ, 2 * KV_RANK), BF16),
            pltpu.VMEM((2 * KV_RANK, TK), BF16),
            pltpu.VMEM((B_HEADS * TQ, LANES), F32),
            pltpu.VMEM((B_HEADS * TQ, 2 * KV_RANK), F32),
            pltpu.VMEM((B_HEADS * TQ, TK), BF16),
            pltpu.VMEM((B_HEADS * TQ, LANES), F32),
            pltpu.VMEM((IDX_HEADS, TQ, IDX_DIM), BF16),
            pltpu.VMEM((IDX_HEADS, TQ, LANES), F32),
        ],
        compiler_params=_cparams(("arbitrary",)),
        name="dsa",
    )(qlat, qidx, widx, kidxt, ckvt, ckvx, biasd, w_uv)


def _merge_kernel(x_ref, hg_ref, ob_ref, ga_ref, gb_ref, wa_ref, wb_ref, wo_ref, fn_ref,
                  x1_ref, h2_ref):
    ya = _dot(hg_ref[...], wa_ref[...])
    yb = _dot(ob_ref[...], wb_ref[...])
    mix = _sigmoid(ga_ref[...]) * ya + _sigmoid(gb_ref[...]) * yb
    x1 = x_ref[...] + _dot(mix.astype(BF16), wo_ref[...])
    x1_ref[...] = x1
    h2 = x1 * lax.rsqrt(jnp.mean(x1 * x1, axis=-1, keepdims=True) + EPS) * fn_ref[...]
    h2_ref[...] = h2.astype(BF16)


def _merge(x2, hg, ob, proj, w_a, w_b, w_o, ffn_norm, tm):
    s, d = x2.shape
    gblk = COL_GATE // d
    row = lambda i: (i, 0)
    return pl.pallas_call(
        _merge_kernel,
        out_shape=[jax.ShapeDtypeStruct((s, d), F32), jax.ShapeDtypeStruct((s, d), BF16)],
        grid=(s // tm,),
        in_specs=[
            pl.BlockSpec((tm, d), row),
            pl.BlockSpec((tm, A_WIDTH), row),
            pl.BlockSpec((tm, B_WIDTH), row),
            pl.BlockSpec((tm, d), lambda i: (i, gblk)),
            pl.BlockSpec((tm, d), lambda i: (i, gblk + 1)),
            _const_spec((A_WIDTH, d)),
            _const_spec((B_WIDTH, d)),
            _const_spec((d, d)),
            _const_spec((1, d)),
        ],
        out_specs=[pl.BlockSpec((tm, d), row), pl.BlockSpec((tm, d), row)],
        compiler_params=_cparams(("arbitrary",)),
        name="merge",
    )(x2, hg, ob, proj, proj, w_a, w_b, w_o, ffn_norm)


def _ffn_kernel(x1_ref, h2_ref, wup_ref, cw_ref, cb_ref, wdn_ref, fin_ref, o_ref,
                tail_scr, u_scr, act_scr, *, cbw):
    tm = h2_ref.shape[0]
    dff = wdn_ref.shape[0]
    halo = 8

    @pl.when(pl.program_id(0) == 0)
    def _():
        tail_scr[...] = jnp.zeros_like(tail_scr)

    h2 = h2_ref[...]

    def conv_cols(c0):
        u = _dot(h2, wup_ref[:, c0:c0 + cbw])
        u_scr[0:halo, :] = tail_scr[:, c0:c0 + cbw]
        u_scr[halo:, :] = u
        tail_scr[:, c0:c0 + cbw] = u[tm - halo:, :]
        cw = cw_ref[:, c0:c0 + cbw]
        return (cw[2:3] * u + cw[1:2] * u_scr[halo - 1:halo - 1 + tm, :]
                + cw[0:1] * u_scr[halo - 2:halo - 2 + tm, :] + cb_ref[:, c0:c0 + cbw])

    for j in range(dff // cbw):
        a = conv_cols(j * cbw)
        v = conv_cols(dff + j * cbw)
        act_scr[:, j * cbw:(j + 1) * cbw] = (a * _sigmoid(a) * v).astype(BF16)

    y = x1_ref[...] + _dot(act_scr[...], wdn_ref[...])
    o_ref[...] = y * lax.rsqrt(jnp.mean(y * y, axis=-1, keepdims=True) + EPS) * fin_ref[...]


def _ffn(x1, h2, w_up, conv_w, conv_b, w_down, final_norm, tm, cbw):
    s, d = x1.shape
    dff = w_down.shape[0]
    row = lambda i: (i, 0)
    return pl.pallas_call(
        functools.partial(_ffn_kernel, cbw=cbw),
        out_shape=jax.ShapeDtypeStruct((s, d), F32),
        grid=(s // tm,),
        in_specs=[
            pl.BlockSpec((tm, d), row),
            pl.BlockSpec((tm, d), row),
            _const_spec((d, 2 * dff)),
            _const_spec((CONV_WIDTH, 2 * dff)),
            _const_spec((1, 2 * dff)),
            _const_spec((dff, d)),
            _const_spec((1, d)),
        ],
        out_specs=pl.BlockSpec((tm, d), row),
        scratch_shapes=[
            pltpu.VMEM((8, 2 * dff), F32),
            pltpu.VMEM((tm + 8, cbw), F32),
            pltpu.VMEM((tm, dff), BF16),
        ],
        compiler_params=_cparams(("arbitrary",)),
        name="ffn",
    )(x1, h2, w_up, conv_w, conv_b, w_down, final_norm)


def _row_tile(s, want):
    t = min(want, s)
    while s % t:
        t //= 2
    return t


def kernel(x, attn_norm, w_in, lower_bounds, hgrn_out_norm, q_norm, kv_norm, w_uq, w_uk, w_uv, w_iq,
           rel_bias, w_branch_a, w_branch_b, w_o, ffn_norm, w_up, conv_w, conv_b, w_down, final_norm):
    bsz, s, d = x.shape
    assert bsz == 1 and attn_norm.shape[0] == 1 and lower_bounds.shape[0] == 2
    x2 = x.reshape(s, d)
    top_k = min(TOPK_MAX, s // 4)

    wi = w_in[0]
    n_small = Q_RANK + KV_RANK + IDX_DIM + IDX_HEADS
    w_pad = jnp.concatenate(
        [wi[:, :COL_GATE], wi[:, COL_GATE + n_small:], wi[:, COL_GATE:COL_GATE + n_small],
         jnp.zeros((d, SMALL_W - n_small), wi.dtype)], axis=1).astype(BF16)

    proj = _inproj(x2, attn_norm, w_pad, _row_tile(s, 1024), N_PROJ // 4)

    hg = _hgrn(proj, lower_bounds, hgrn_out_norm, _row_tile(s, 512))

    qlat, qidx, widx, ckvx, ckvt, kidxt = _mla_prep(
        proj, q_norm, kv_norm, w_uq[0].astype(BF16), jnp.swapaxes(w_uk[0], 1, 2).astype(BF16),
        w_iq[0].astype(BF16), _row_tile(s, 512))

    t_loc = jnp.arange(TQ, dtype=jnp.int32)[None, :, None]
    s_loc = jnp.arange(TKN, dtype=jnp.int32)[None, None, :]
    off = jnp.array([0, -TQ], jnp.int32)[:, None, None]
    buckets = _t5_bucket(s_loc - t_loc + off)
    biasd = _bias_tab(buckets, rel_bias)

    ob = _dsa(qlat, qidx, widx, kidxt, ckvt, ckvx, biasd, w_uv[0].astype(BF16), top_k)

    x1, h2 = _merge(x2, hg, ob, proj, w_branch_a[0].astype(BF16), w_branch_b[0].astype(BF16),
                    w_o[0].astype(BF16), ffn_norm, _row_tile(s, 512))

    out = _ffn(x1, h2, w_up[0].astype(BF16), conv_w[0], conv_b, w_down[0].astype(BF16),
               final_norm.reshape(1, d), _row_tile(s, 512), 256)
    return out.reshape(bsz, s, d)
```

```python
import functools
import math

import jax
import jax.numpy as jnp
from jax import lax
from jax.experimental import pallas as pl
from jax.experimental.pallas import tpu as pltpu

F32 = jnp.float32
BF16 = jnp.bfloat16
I32 = jnp.int32

EPS = 1e-6
CHUNK = 64
A_HEADS = 8
A_HEAD_DIM = 128
A_WIDTH = A_HEADS * A_HEAD_DIM
B_HEADS = 16
B_QK_DIM = 64
B_V_DIM = 64
B_WIDTH = B_HEADS * B_V_DIM
Q_RANK = 256
KV_RANK = 128
IDX_HEADS = 8
IDX_DIM = 64
TOPK_MAX = 256
N_BUCKETS = 32
MAX_DISTANCE = 128
CONV_WIDTH = 3

LANES = 128
VMEM_LIMIT = 56 * 1024 * 1024

COL_QFIG = 0
COL_GATE = 4 * A_WIDTH
COL_SMALL = COL_GATE + 2048
SMALL_W = 512
N_PROJ = COL_SMALL + SMALL_W

TQ = 128
TKA = 512
TK = 512
TKN = 2 * TQ
NEG = -1e30
EXP_SAFE = 80.0
MAX_BISECT = 1024
LOG2E = 1.4426950408889634


def _cparams(sem):
    return pltpu.CompilerParams(dimension_semantics=sem, vmem_limit_bytes=VMEM_LIMIT)


def _const_spec(shape):
    nd = len(shape)
    return pl.BlockSpec(shape, lambda *_: (0,) * nd, pipeline_mode=pl.Buffered(1))


def _dot(a, b):
    return jnp.dot(a, b, preferred_element_type=F32)


def _dot_nt(a, b):
    return lax.dot_general(a, b, (((1,), (1,)), ((), ())), preferred_element_type=F32)


def _dot_tn(a, b):
    return lax.dot_general(a, b, (((0,), (0,)), ((), ())), preferred_element_type=F32)


def _sigmoid(x):
    return jax.nn.sigmoid(x)


def _inproj_kernel(x_ref, g_ref, w_ref, o_ref, h_scr):
    @pl.when(pl.program_id(1) == 0)
    def _():
        x = x_ref[...]
        y = x * lax.rsqrt(jnp.mean(x * x, axis=-1, keepdims=True) + EPS) * g_ref[...]
        h_scr[...] = y.astype(BF16)

    o_ref[...] = _dot(h_scr[...], w_ref[...])


def _inproj(x2, gain, w_pad, tm, tn):
    s, d = x2.shape
    n = w_pad.shape[1]
    return pl.pallas_call(
        _inproj_kernel,
        out_shape=jax.ShapeDtypeStruct((s, n), F32),
        grid=(s // tm, n // tn),
        in_specs=[
            pl.BlockSpec((tm, d), lambda i, j: (i, 0)),
            pl.BlockSpec((1, d), lambda i, j: (0, 0)),
            pl.BlockSpec((d, tn), lambda i, j: (0, j)),
        ],
        out_specs=pl.BlockSpec((tm, tn), lambda i, j: (i, j)),
        scratch_shapes=[pltpu.VMEM((tm, d), BF16)],
        compiler_params=_cparams(("arbitrary", "arbitrary")),
        name="inproj",
    )(x2, gain, w_pad)


def _hgrn_kernel(q_ref, f_ref, i_ref, g_ref, lbp_ref, on_ref, o_ref,
                 st_ref, b_scr, k_scr, qs_scr, att_scr, *, nch):
    c_ = CHUNK
    half = c_ // 2

    @pl.when(pl.program_id(1) == 0)
    def _():
        st_ref[...] = jnp.zeros_like(st_ref)

    l0 = lbp_ref[0:1, :]
    l1 = lbp_ref[1:2, :]
    lm = jnp.maximum(l0, l1)
    e0 = jnp.exp(l0 - lm)
    e1 = jnp.exp(l1 - lm)
    lb = e0 / (e0 + e1)

    row = lax.broadcasted_iota(I32, (c_, c_), 0)
    col = lax.broadcasted_iota(I32, (c_, c_), 1)
    tri = row >= col
    tri_bf = jnp.where(tri, 1.0, 0.0).astype(BF16)

    q = q_ref[...]
    qs_scr[...] = q * _sigmoid(q)
    fg = lb + (1.0 - lb) * _sigmoid(f_ref[...])
    k_scr[...] = 1.0 - fg
    lg = jnp.log(fg)

    d = A_HEAD_DIM
    hpg = q_ref.shape[1] // d
    heads = [slice(hh * d, (hh + 1) * d) for hh in range(hpg)]

    dev = jnp.zeros((1, 1), F32)
    for c in range(nch):
        l = lg[c * c_:(c + 1) * c_]
        hi = l.astype(BF16)
        r1 = l - hi.astype(F32)
        mid = r1.astype(BF16)
        lo = (r1 - mid.astype(F32)).astype(BF16)
        b = _dot(tri_bf, hi) + _dot(tri_bf, mid) + _dot(tri_bf, lo)
        b_scr[c * c_:(c + 1) * c_, :] = b
        rho = b[half - 1:half, :]
        dev = jnp.maximum(dev, jnp.max(jnp.abs(b - rho), keepdims=True))
    safe = dev[0, 0] <= EXP_SAFE

    @pl.when(safe)
    def _():
        for c in range(nch):
            sl = slice(c * c_, (c + 1) * c_)
            b = b_scr[sl, :]
            rho = b[half - 1:half, :]
            qt = (qs_scr[sl, :] * jnp.exp(b - rho)).astype(BF16)
            kt = (k_scr[sl, :] * jnp.exp(rho - b)).astype(BF16)
            for hh, hs in enumerate(heads):
                att_scr[hh, c] = jnp.where(tri, _dot_nt(qt[:, hs], kt[:, hs]), 0.0)

    @pl.when(jnp.logical_not(safe))
    def _():
        trow = lax.broadcasted_iota(I32, (c_, d), 0)
        for c in range(nch):
            sl = slice(c * c_, (c + 1) * c_)
            for hh, hs in enumerate(heads):
                b = b_scr[sl, hs]
                qv = qs_scr[sl, hs]

                def body(s, att):
                    bs = b_scr[pl.ds(c * c_ + s, 1), :][:, hs]
                    ks = k_scr[pl.ds(c * c_ + s, 1), :][:, hs]
                    dd = jnp.where(trow >= s, b - bs, -jnp.inf)
                    colv = jnp.sum(qv * ks * jnp.exp(dd), axis=-1, keepdims=True)
                    return jnp.where(col == s, colv, att)

                att_scr[hh, c] = lax.fori_loop(0, c_, body, jnp.zeros((c_, c_), F32))

    on = on_ref[...]
    for c in range(nch):
        sl = slice(c * c_, (c + 1) * c_)
        b = b_scr[sl, :]
        bend = b[c_ - 1:c_, :]
        v = i_ref[sl, :].astype(BF16)
        qe = (qs_scr[sl, :] * jnp.exp(b)).astype(BF16)
        kd = (k_scr[sl, :] * jnp.exp(bend - b)).astype(BF16)
        decay = jnp.exp(bend)
        g = g_ref[sl, :]
        gate = g * _sigmoid(g)
        for hh, hs in enumerate(heads):
            st = st_ref[hh]
            o = _dot_nt(qe[:, hs], st.astype(BF16)) + _dot(att_scr[hh, c].astype(BF16), v[:, hs])
            st_ref[hh] = st * decay[:, hs] + _dot_tn(v[:, hs], kd[:, hs])
            o = o * lax.rsqrt(jnp.mean(o * o, axis=-1, keepdims=True) + EPS) * on
            o_ref[sl, hs] = (o * gate[:, hs]).astype(o_ref.dtype)


def _hgrn(proj, lower_bounds, out_norm, t_rows, hpg):
    s = proj.shape[0]
    nch = t_rows // CHUNK
    d = A_HEAD_DIM
    w = hpg * d
    nb = COL_QFIG // w

    def colspec(off):
        return pl.BlockSpec((t_rows, w), lambda h, i, off=off: (i, nb + off * (A_HEADS // hpg) + h))

    return pl.pallas_call(
        functools.partial(_hgrn_kernel, nch=nch),
        out_shape=jax.ShapeDtypeStruct((s, A_WIDTH), BF16),
        grid=(A_HEADS // hpg, s // t_rows),
        in_specs=[
            colspec(0), colspec(1), colspec(2), colspec(3),
            pl.BlockSpec((2, w), lambda h, i: (0, h)),
            pl.BlockSpec((1, d), lambda h, i: (0, 0)),
        ],
        out_specs=pl.BlockSpec((t_rows, w), lambda h, i: (i, h)),
        scratch_shapes=[
            pltpu.VMEM((hpg, d, d), F32),
            pltpu.VMEM((t_rows, w), F32),
            pltpu.VMEM((t_rows, w), F32),
            pltpu.VMEM((t_rows, w), F32),
            pltpu.VMEM((hpg, nch, CHUNK, CHUNK), F32),
        ],
        compiler_params=_cparams(("arbitrary", "arbitrary")),
        name="hgrn",
    )(proj, proj, proj, proj, lower_bounds, out_norm)


def _mla_prep_kernel(p_ref, qn_ref, kn_ref, wuq_ref, wukt_ref, wiq_ref,
                     qlat_ref, qidx_ref, widx_ref, ckvx_ref, ckvt_ref, kidxt_ref):
    p = p_ref[...]
    cq = p[:, 0:Q_RANK]
    cq = cq * lax.rsqrt(jnp.mean(cq * cq, axis=-1, keepdims=True) + EPS) * qn_ref[...]
    cqb = cq.astype(BF16)
    ckv = p[:, Q_RANK:Q_RANK + KV_RANK]
    ckv = ckv * lax.rsqrt(jnp.mean(ckv * ckv, axis=-1, keepdims=True) + EPS) * kn_ref[...]

    q = _dot(cqb, wuq_ref[...])
    scale = B_QK_DIM ** -0.5 * LOG2E
    for h in range(B_HEADS):
        qh = q[:, h * B_QK_DIM:(h + 1) * B_QK_DIM].astype(BF16)
        qlat_ref[:, h * KV_RANK:(h + 1) * KV_RANK] = (_dot(qh, wukt_ref[h]) * scale).astype(BF16)
    qidx_ref[...] = _dot(cqb, wiq_ref[...]).astype(BF16)

    rest = p[:, Q_RANK + KV_RANK:]
    widx_ref[...] = rest[:, IDX_DIM:IDX_DIM + IDX_HEADS] * ((IDX_HEADS * IDX_DIM) ** -0.5)
    ckvx_ref[:, 0:KV_RANK] = ckv.astype(BF16)
    ckvx_ref[:, KV_RANK:] = jnp.ones((p.shape[0], KV_RANK), BF16)
    ckvt_ref[...] = ckv.T.astype(BF16)
    kidxt_ref[...] = rest.T[0:IDX_DIM, :].astype(BF16)


def _mla_prep(proj, q_norm, kv_norm, w_uq, w_ukt, w_iq, tm):
    s = proj.shape[0]
    return pl.pallas_call(
        _mla_prep_kernel,
        out_shape=[
            jax.ShapeDtypeStruct((s, B_HEADS * KV_RANK), BF16),
            jax.ShapeDtypeStruct((s, IDX_HEADS * IDX_DIM), BF16),
            jax.ShapeDtypeStruct((s, IDX_HEADS), F32),
            jax.ShapeDtypeStruct((s, 2 * KV_RANK), BF16),
            jax.ShapeDtypeStruct((KV_RANK, s), BF16),
            jax.ShapeDtypeStruct((IDX_DIM, s), BF16),
        ],
        grid=(s // tm,),
        in_specs=[
            pl.BlockSpec((tm, SMALL_W), lambda i: (i, COL_SMALL // SMALL_W)),
            _const_spec((1, Q_RANK)),
            _const_spec((1, KV_RANK)),
            _const_spec((Q_RANK, B_HEADS * B_QK_DIM)),
            _const_spec((B_HEADS, B_QK_DIM, KV_RANK)),
            _const_spec((Q_RANK, IDX_HEADS * IDX_DIM)),
        ],
        out_specs=[
            pl.BlockSpec((tm, B_HEADS * KV_RANK), lambda i: (i, 0)),
            pl.BlockSpec((tm, IDX_HEADS * IDX_DIM), lambda i: (i, 0)),
            pl.BlockSpec((tm, IDX_HEADS), lambda i: (i, 0)),
            pl.BlockSpec((tm, 2 * KV_RANK), lambda i: (i, 0)),
            pl.BlockSpec((KV_RANK, tm), lambda i: (0, i)),
            pl.BlockSpec((IDX_DIM, tm), lambda i: (0, i)),
        ],
        compiler_params=_cparams(("arbitrary",)),
        name="mla_prep",
    )(proj, q_norm, kv_norm, w_uq, w_ukt, w_iq)


def _bias_tab_kernel(bkt_ref, rb_ref, o_ref):
    far = N_BUCKETS // 2 - 1
    for v in range(2):
        bkt = bkt_ref[v]
        for h in range(B_HEADS):
            acc = jnp.zeros(bkt.shape, F32)
            for k in range(N_BUCKETS):
                acc = jnp.where(bkt == k, (rb_ref[k, h] - rb_ref[far, h]) * LOG2E, acc)
            o_ref[v, h] = acc


def _bias_tab(buckets, rel_bias):
    return pl.pallas_call(
        _bias_tab_kernel,
        out_shape=jax.ShapeDtypeStruct((2, B_HEADS, TQ, TKN), F32),
        in_specs=[
            pl.BlockSpec(memory_space=pltpu.VMEM),
            pl.BlockSpec(memory_space=pltpu.SMEM),
        ],
        out_specs=pl.BlockSpec(memory_space=pltpu.VMEM),
        compiler_params=pltpu.CompilerParams(vmem_limit_bytes=VMEM_LIMIT),
        name="bias_tab",
    )(buckets, rel_bias)


def _t5_bucket(rel):
    half = N_BUCKETS // 2
    max_exact = half // 2
    ret = jnp.where(rel > 0, half, 0)
    n = jnp.abs(rel)
    nf = jnp.maximum(n, 1).astype(jnp.float32)
    large = max_exact + (jnp.log(nf / max_exact) / math.log(MAX_DISTANCE / max_exact) * (half - max_exact)).astype(jnp.int32)
    large = jnp.minimum(large, half - 1)
    return ret + jnp.where(n < max_exact, n, large)


def _dsa_kernel(qlat_ref, qidx_ref, widx_ref, kidxt_ref, ckvt_ref, ckvx_ref, biasd_ref, wuv_ref,
                o_ref, isc_scr, t_scr, lhs_scr, rhs_scr, m_scr, acc_scr, p_scr, al_scr, qh_scr, wb_scr,
                *, top_k):
    qi = pl.program_id(0)
    q0 = qi * TQ
    kf = float(top_k)
    lane = lax.broadcasted_iota(I32, (TQ, LANES), 1)
    rowi = lax.broadcasted_iota(I32, (TQ, LANES), 0)
    lim = q0 + (rowi // CHUNK + 1) * CHUNK

    na = (q0 + TQ + TKA - 1) // TKA
    qidx = qidx_ref[...]
    w = widx_ref[...]
    for h in range(IDX_HEADS):
        qh_scr[h] = qidx[:, h * IDX_DIM:(h + 1) * IDX_DIM]
        wb_scr[h] = jnp.broadcast_to(w[:, h:h + 1], (TQ, LANES))

    def lanes_x(v, width):
        return jnp.concatenate([v] * (width // LANES), axis=1)

    def scores(k0):
        kt = kidxt_ref[:, pl.ds(k0, TKA)]
        acc = jnp.zeros((TQ, TKA), F32)
        for h in range(IDX_HEADS):
            acc = acc + jnp.maximum(_dot(qh_scr[h], kt), 0.0) * lanes_x(wb_scr[h], TKA)
        return acc

    def emit(k0, acc, stats):
        mn, mx, c_ge0, c_gt0 = stats
        for g in range(TKA // LANES):
            a = acc[:, g * LANES:(g + 1) * LANES]
            mn = jnp.minimum(mn, a)
            mx = jnp.maximum(mx, a)
            adm = (k0 + g * LANES + lane) < lim
            a = jnp.where(adm, a, -jnp.inf)
            c_ge0 = c_ge0 + jnp.where(a >= 0.0, 1.0, 0.0)
            c_gt0 = c_gt0 + jnp.where(a > 0.0, 1.0, 0.0)
            isc_scr[:, pl.ds(k0 + g * LANES, LANES)] = a
        return mn, mx, c_ge0, c_gt0

    def phase_a(jb, stats):
        k0 = pl.multiple_of(jb * (2 * TKA), 2 * TKA)
        a0 = scores(k0)
        a1 = scores(k0 + TKA)
        return emit(k0 + TKA, a1, emit(k0, a0, stats))

    zeros_q = jnp.zeros((TQ, LANES), F32)
    mn, mx, pc_ge0, pc_gt0 = lax.fori_loop(
        0, (na + 1) // 2, phase_a,
        (jnp.full((TQ, LANES), jnp.inf, F32), jnp.full((TQ, LANES), -jnp.inf, F32), zeros_q, zeros_q))
    ng = na * (TKA // LANES)

    def count(pred):
        def body(g, cnt):
            c0 = pl.multiple_of(g * LANES, LANES)
            x = isc_scr[:, pl.ds(c0, LANES)]
            return cnt + jnp.where(pred(x, c0 + lane), 1.0, 0.0)
        cnt = lax.fori_loop(0, ng, body, jnp.zeros((TQ, LANES), F32))
        return jnp.broadcast_to(jnp.sum(cnt, axis=1, keepdims=True), (TQ, LANES))

    t_scr[...] = jnp.full((TQ, LANES), -jnp.inf, F32)

    @pl.when(q0 + CHUNK > top_k)
    def _():
        def row_total(part):
            return jnp.broadcast_to(jnp.sum(part, axis=1, keepdims=True), (TQ, LANES))

        rmin = jnp.broadcast_to(jnp.min(mn, axis=1, keepdims=True), (TQ, LANES))
        rmax = jnp.broadcast_to(jnp.max(mx, axis=1, keepdims=True), (TQ, LANES))
        n_adm = lim.astype(F32)
        c_ge0, c_gt0 = row_total(pc_ge0), row_total(pc_gt0)
        above = c_gt0 >= kf
        below = c_ge0 < kf
        tiny = float(jnp.finfo(F32).tiny)
        hi_top = rmax + (rmax - rmin) + jnp.abs(rmax) * 1e-3 + 1e-30
        lo0 = jnp.where(below, rmin, 0.0)
        clo0 = jnp.where(below, n_adm, c_ge0)
        hi0 = jnp.where(below, 0.0, jnp.where(above, hi_top, tiny))
        chi0 = jnp.where(below, c_ge0, jnp.where(above, 0.0, c_gt0))
        log_k = math.log(kf)

        def midpoint(lo, hi):
            return 0.5 * lo + 0.5 * hi

        def is_open(lo, hi):
            mid = midpoint(lo, hi)
            return jnp.logical_and(mid > lo, mid < hi)

        def n_active(lo, hi, done):
            act = jnp.logical_and(done == 0.0, is_open(lo, hi))
            return jnp.sum(jnp.where(act, 1.0, 0.0))

        def pivot(it, lo, hi, clo, chi):
            llo = jnp.log(clo)
            lhi = jnp.log(jnp.maximum(chi, 0.5))
            t = jnp.clip((llo - log_k) * pl.reciprocal(llo - lhi, approx=True), 0.03, 0.97)
            guess = lo + t * (hi - lo)
            mid = midpoint(lo, hi)
            use = jnp.logical_and(jnp.logical_and(guess > lo, guess < hi), it % 3 != 2)
            return jnp.where(use, guess, mid)

        def count_ge(mid):
            w = 4 * LANES
            mid_w = lanes_x(mid, w)

            def body(g, cnt):
                c0 = pl.multiple_of(g * w, w)
                ind = jnp.where(isc_scr[:, pl.ds(c0, w)] >= mid_w, 1.0, 0.0)
                return cnt + ((ind[:, 0:LANES] + ind[:, LANES:2 * LANES])
                              + (ind[:, 2 * LANES:3 * LANES] + ind[:, 3 * LANES:]))

            return row_total(lax.fori_loop(0, ng // 4, body, jnp.zeros((TQ, LANES), F32)))

        def cond(st):
            it, nact = st[0], st[1]
            return jnp.logical_and(it < MAX_BISECT, nact > 0.0)

        def body(st):
            it, nact, lo, hi, clo, chi, done = st
            act = jnp.logical_and(done == 0.0, is_open(lo, hi))
            mid = pivot(it, lo, hi, clo, chi)
            c = count_ge(mid)
            up = jnp.logical_and(act, c >= kf)
            dn = jnp.logical_and(act, c < kf)
            lo = jnp.where(up, mid, lo)
            clo = jnp.where(up, c, clo)
            hi = jnp.where(dn, mid, hi)
            chi = jnp.where(dn, c, chi)
            done = jnp.where(jnp.logical_and(act, c == kf), 1.0, done)
            return it + 1, n_active(lo, hi, done), lo, hi, clo, chi, done

        done0 = jnp.where(c_ge0 == kf, 1.0, 0.0)
        _, _, t_lo, t_hi, _, _, done = lax.while_loop(
            cond, body, (jnp.int32(0), n_active(lo0, hi0, done0), lo0, hi0, clo0, chi0, done0))
        t_scr[...] = t_lo

        n_tie = jnp.sum(jnp.where(done == 0.0, 1.0, 0.0))

        @pl.when(n_tie > 0.0)
        def _():
            lo, hi, tie = t_lo, t_hi, done == 0.0
            r = kf - count(lambda x, idx: x >= hi)
            p = jnp.zeros((TQ, LANES), I32)
            nbits = max(1, (isc_scr.shape[1] - 1).bit_length())
            for bit in range(nbits - 1, -1, -1):
                cand = p + (1 << bit)
                f = count(lambda x, idx: jnp.logical_and(jnp.logical_and(x >= lo, x < hi), idx < cand))
                p = jnp.where(f < r, cand, p)

            def drop(g, carry):
                c0 = pl.multiple_of(g * LANES, LANES)
                x = isc_scr[:, pl.ds(c0, LANES)]
                ex = jnp.logical_and(jnp.logical_and(tie, (c0 + lane) > p),
                                     jnp.logical_and(x >= lo, x < hi))
                isc_scr[:, pl.ds(c0, LANES)] = jnp.where(ex, -jnp.inf, x)
                return carry

            lax.fori_loop(0, ng, drop, 0)

    thr = t_scr[...]
    for h in range(B_HEADS):
        lhs_scr[h * TQ:(h + 1) * TQ, 0:KV_RANK] = qlat_ref[:, h * KV_RANK:(h + 1) * KV_RANK]

    @pl.when(qi == 0)
    def _():
        eye = jnp.where(lax.broadcasted_iota(I32, (TQ, TQ), 0) == lax.broadcasted_iota(I32, (TQ, TQ), 1),
                        1.0, 0.0).astype(BF16)
        for h in range(B_HEADS):
            lhs_scr[h * TQ:(h + 1) * TQ, KV_RANK:] = eye

    m_scr[...] = jnp.full(m_scr.shape, NEG, F32)
    acc_scr[...] = jnp.zeros_like(acc_scr)

    def accumulate(h, k0_prev, w_prev):
        rs = slice(h * TQ, (h + 1) * TQ)
        kv = ckvx_ref[pl.ds(k0_prev, w_prev), :]
        acc_scr[rs, :] = (acc_scr[rs, :] * lanes_x(al_scr[rs, :], 2 * KV_RANK)
                          + _dot(p_scr[rs, 0:w_prev], kv))

    def stage(k0_prev, w_prev, k0, width, key_lim, with_bias):
        x = isc_scr[:, pl.ds(k0, width)]
        kidx = k0 + lax.broadcasted_iota(I32, (TQ, width), 1)
        sel = jnp.logical_and(x >= lanes_x(thr, width), kidx < key_lim)
        rhs_scr[0:KV_RANK, 0:width] = ckvt_ref[:, pl.ds(k0, width)]
        rhs_scr[KV_RANK:, 0:width] = jnp.where(sel, 0.0, NEG).astype(BF16)
        rhs = rhs_scr[:, 0:width]
        for h in range(B_HEADS):
            rs = slice(h * TQ, (h + 1) * TQ)
            if w_prev:
                accumulate(h, k0_prev, w_prev)
            s = _dot(lhs_scr[rs, :], rhs)
            if with_bias:
                s = s + biasd_ref[0, h]
            m_old = m_scr[rs, :]
            m_new = jnp.maximum(m_old, jnp.max(s, axis=1, keepdims=True))
            al_scr[rs, :] = jnp.exp2(m_old - m_new)
            p_scr[rs, 0:width] = jnp.exp2(s - lanes_x(m_new, width)).astype(BF16)
            m_scr[rs, :] = m_new

    far_lim = jnp.maximum(q0 - TQ, 0)
    nf = (far_lim + TK - 1) // TK

    @pl.when(nf == 0)
    def _():
        p_scr[...] = jnp.zeros_like(p_scr)
        al_scr[...] = jnp.ones_like(al_scr)

    @pl.when(nf > 0)
    def _():
        stage(0, 0, 0, TK, far_lim, False)

    def far_block(jb, carry):
        stage(pl.multiple_of((jb - 1) * TK, TK), TK, pl.multiple_of(jb * TK, TK), TK, far_lim, False)
        return carry

    lax.fori_loop(1, nf, far_block, 0)
    k0_last = pl.multiple_of(jnp.maximum(nf - 1, 0) * TK, TK)
    k0_near = pl.multiple_of(far_lim, TQ)
    stage(k0_last, TK, k0_near, TKN, lanes_x(lim, TKN), True)
    for h in range(B_HEADS):
        accumulate(h, k0_near, TKN)

    for h in range(B_HEADS):
        rs = slice(h * TQ, (h + 1) * TQ)
        a = acc_scr[rs, :]
        olat = (a[:, 0:KV_RANK] / a[:, KV_RANK:]).astype(BF16)
        o_ref[:, h * B_V_DIM:(h + 1) * B_V_DIM] = _dot(olat, wuv_ref[h]).astype(o_ref.dtype)


def _dsa(qlat, qidx, widx, kidxt, ckvt, ckvx, biasd, w_uv, top_k):
    s = qlat.shape[0]
    assert s % (2 * TKA) == 0 and TK == TKA and top_k % TQ == 0
    return pl.pallas_call(
        functools.partial(_dsa_kernel, top_k=top_k),
        out_shape=jax.ShapeDtypeStruct((s, B_WIDTH), BF16),
        grid=(s // TQ,),
        in_specs=[
            pl.BlockSpec((TQ, B_HEADS * KV_RANK), lambda i: (i, 0)),
            pl.BlockSpec((TQ, IDX_HEADS * IDX_DIM), lambda i: (i, 0)),
            pl.BlockSpec((TQ, IDX_HEADS), lambda i: (i, 0)),
            _const_spec((IDX_DIM, s)),
            _const_spec((KV_RANK, s)),
            _const_spec((s, 2 * KV_RANK)),
            pl.BlockSpec((1, B_HEADS, TQ, TKN), lambda i: (jnp.minimum(i, 1), 0, 0, 0)),
            _const_spec((B_HEADS, KV_RANK, B_V_DIM)),
        ],
        out_specs=pl.BlockSpec((TQ, B_WIDTH), lambda i: (i, 0)),
        scratch_shapes=[
            pltpu.VMEM((TQ, s), F32),
            pltpu.VMEM((TQ, LANES), F32),
            pltpu.VMEM((B_HEADS * TQ, 2 * KV_RANK), BF16),
            pltpu.VMEM((2 * KV_RANK, TK), BF16),
            pltpu.VMEM((B_HEADS * TQ, LANES), F32),
            pltpu.VMEM((B_HEADS * TQ, 2 * KV_RANK), F32),
            pltpu.VMEM((B_HEADS * TQ, TK), BF16),
            pltpu.VMEM((B_HEADS * TQ, LANES), F32),
            pltpu.VMEM((IDX_HEADS, TQ, IDX_DIM), BF16),
            pltpu.VMEM((IDX_HEADS, TQ, LANES), F32),
        ],
        compiler_params=_cparams(("arbitrary",)),
        name="dsa",
    )(qlat, qidx, widx, kidxt, ckvt, ckvx, biasd, w_uv)


def _merge_kernel(x_ref, hg_ref, ob_ref, ga_ref, gb_ref, wa_ref, wb_ref, wo_ref, fn_ref,
                  x1_ref, h2_ref):
    ya = _dot(hg_ref[...], wa_ref[...])
    yb = _dot(ob_ref[...], wb_ref[...])
    mix = _sigmoid(ga_ref[...]) * ya + _sigmoid(gb_ref[...]) * yb
    x1 = x_ref[...] + _dot(mix.astype(BF16), wo_ref[...])
    x1_ref[...] = x1
    h2 = x1 * lax.rsqrt(jnp.mean(x1 * x1, axis=-1, keepdims=True) + EPS) * fn_ref[...]
    h2_ref[...] = h2.astype(BF16)


def _merge(x2, hg, ob, proj, w_a, w_b, w_o, ffn_norm, tm):
    s, d = x2.shape
    gblk = COL_GATE // d
    row = lambda i: (i, 0)
    return pl.pallas_call(
        _merge_kernel,
        out_shape=[jax.ShapeDtypeStruct((s, d), F32), jax.ShapeDtypeStruct((s, d), BF16)],
        grid=(s // tm,),
        in_specs=[
            pl.BlockSpec((tm, d), row),
            pl.BlockSpec((tm, A_WIDTH), row),
            pl.BlockSpec((tm, B_WIDTH), row),
            pl.BlockSpec((tm, d), lambda i: (i, gblk)),
            pl.BlockSpec((tm, d), lambda i: (i, gblk + 1)),
            _const_spec((A_WIDTH, d)),
            _const_spec((B_WIDTH, d)),
            _const_spec((d, d)),
            _const_spec((1, d)),
        ],
        out_specs=[pl.BlockSpec((tm, d), row), pl.BlockSpec((tm, d), row)],
        compiler_params=_cparams(("arbitrary",)),
        name="merge",
    )(x2, hg, ob, proj, proj, w_a, w_b, w_o, ffn_norm)


def _ffn_kernel(x1_ref, h2_ref, wup_ref, cw_ref, cb_ref, wdn_ref, fin_ref, o_ref,
                tail_scr, u_scr, act_scr, *, cbw):
    tm = h2_ref.shape[0]
    dff = wdn_ref.shape[0]
    halo = 8

    @pl.when(pl.program_id(0) == 0)
    def _():
        tail_scr[...] = jnp.zeros_like(tail_scr)

    h2 = h2_ref[...]

    def conv_cols(c0):
        u = _dot(h2, wup_ref[:, c0:c0 + cbw])
        u_scr[0:halo, :] = tail_scr[:, c0:c0 + cbw]
        u_scr[halo:, :] = u
        tail_scr[:, c0:c0 + cbw] = u[tm - halo:, :]
        cw = cw_ref[:, c0:c0 + cbw]
        return (cw[2:3] * u + cw[1:2] * u_scr[halo - 1:halo - 1 + tm, :]
                + cw[0:1] * u_scr[halo - 2:halo - 2 + tm, :] + cb_ref[:, c0:c0 + cbw])

    for j in range(dff // cbw):
        a = conv_cols(j * cbw)
        v = conv_cols(dff + j * cbw)
        act_scr[:, j * cbw:(j + 1) * cbw] = (a * _sigmoid(a) * v).astype(BF16)

    y = x1_ref[...] + _dot(act_scr[...], wdn_ref[...])
    o_ref[...] = y * lax.rsqrt(jnp.mean(y * y, axis=-1, keepdims=True) + EPS) * fin_ref[...]


def _ffn(x1, h2, w_up, conv_w, conv_b, w_down, final_norm, tm, cbw):
    s, d = x1.shape
    dff = w_down.shape[0]
    row = lambda i: (i, 0)
    return pl.pallas_call(
        functools.partial(_ffn_kernel, cbw=cbw),
        out_shape=jax.ShapeDtypeStruct((s, d), F32),
        grid=(s // tm,),
        in_specs=[
            pl.BlockSpec((tm, d), row),
            pl.BlockSpec((tm, d), row),
            _const_spec((d, 2 * dff)),
            _const_spec((CONV_WIDTH, 2 * dff)),
            _const_spec((1, 2 * dff)),
            _const_spec((dff, d)),
            _const_spec((1, d)),
        ],
        out_specs=pl.BlockSpec((tm, d), row),
        scratch_shapes=[
            pltpu.VMEM((8, 2 * dff), F32),
            pltpu.VMEM((tm + 8, cbw), F32),
            pltpu.VMEM((tm, dff), BF16),
        ],
        compiler_params=_cparams(("arbitrary",)),
        name="ffn",
    )(x1, h2, w_up, conv_w, conv_b, w_down, final_norm)


def _row_tile(s, want):
    t = min(want, s)
    while s % t:
        t //= 2
    return t


def kernel(x, attn_norm, w_in, lower_bounds, hgrn_out_norm, q_norm, kv_norm, w_uq, w_uk, w_uv, w_iq,
           rel_bias, w_branch_a, w_branch_b, w_o, ffn_norm, w_up, conv_w, conv_b, w_down, final_norm):
    bsz, s, d = x.shape
    assert bsz == 1 and attn_norm.shape[0] == 1 and lower_bounds.shape[0] == 2
    x2 = x.reshape(s, d)
    top_k = min(TOPK_MAX, s // 4)

    wi = w_in[0]
    n_small = Q_RANK + KV_RANK + IDX_DIM + IDX_HEADS
    w_pad = jnp.concatenate(
        [wi[:, :COL_GATE], wi[:, COL_GATE + n_small:], wi[:, COL_GATE:COL_GATE + n_small],
         jnp.zeros((d, SMALL_W - n_small), wi.dtype)], axis=1).astype(BF16)

    proj = _inproj(x2, attn_norm, w_pad, _row_tile(s, 1024), N_PROJ // 4)

    hg = _hgrn(proj, lower_bounds, hgrn_out_norm, _row_tile(s, 512), 4)

    qlat, qidx, widx, ckvx, ckvt, kidxt = _mla_prep(
        proj, q_norm, kv_norm, w_uq[0].astype(BF16), jnp.swapaxes(w_uk[0], 1, 2).astype(BF16),
        w_iq[0].astype(BF16), _row_tile(s, 512))

    t_loc = jnp.arange(TQ, dtype=jnp.int32)[None, :, None]
    s_loc = jnp.arange(TKN, dtype=jnp.int32)[None, None, :]
    off = jnp.array([0, -TQ], jnp.int32)[:, None, None]
    buckets = _t5_bucket(s_loc - t_loc + off)
    biasd = _bias_tab(buckets, rel_bias)

    ob = _dsa(qlat, qidx, widx, kidxt, ckvt, ckvx, biasd, w_uv[0].astype(BF16), top_k)

    x1, h2 = _merge(x2, hg, ob, proj, w_branch_a[0].astype(BF16), w_branch_b[0].astype(BF16),
                    w_o[0].astype(BF16), ffn_norm, _row_tile(s, 512))

    out = _ffn(x1, h2, w_up[0].astype(BF16), conv_w[0], conv_b, w_down[0].astype(BF16),
               final_norm.reshape(1, d), _row_tile(s, 512), 256)
    return out.reshape(bsz, s, d)
```

```python
import functools
import math

import jax
import jax.numpy as jnp
from jax import lax
from jax.experimental import pallas as pl
from jax.experimental.pallas import tpu as pltpu

F32 = jnp.float32
BF16 = jnp.bfloat16
I32 = jnp.int32

EPS = 1e-6
CHUNK = 64
A_HEADS = 8
A_HEAD_DIM = 128
A_WIDTH = A_HEADS * A_HEAD_DIM
B_HEADS = 16
B_QK_DIM = 64
B_V_DIM = 64
B_WIDTH = B_HEADS * B_V_DIM
Q_RANK = 256
KV_RANK = 128
IDX_HEADS = 8
IDX_DIM = 64
TOPK_MAX = 256
N_BUCKETS = 32
MAX_DISTANCE = 128
CONV_WIDTH = 3

LANES = 128
VMEM_LIMIT = 56 * 1024 * 1024

COL_QFIG = 0
COL_GATE = 4 * A_WIDTH
COL_SMALL = COL_GATE + 2048
SMALL_W = 512
N_PROJ = COL_SMALL + SMALL_W

TQ = 128
TKA = 512
TK = 512
TKN = 2 * TQ
NEG = -1e30
EXP_SAFE = 80.0
MAX_BISECT = 1024
LOG2E = 1.4426950408889634


def _cparams(sem):
    return pltpu.CompilerParams(dimension_semantics=sem, vmem_limit_bytes=VMEM_LIMIT)


def _const_spec(shape):
    nd = len(shape)
    return pl.BlockSpec(shape, lambda *_: (0,) * nd, pipeline_mode=pl.Buffered(1))


def _dot(a, b):
    return jnp.dot(a, b, preferred_element_type=F32)


def _dot_nt(a, b):
    return lax.dot_general(a, b, (((1,), (1,)), ((), ())), preferred_element_type=F32)


def _dot_tn(a, b):
    return lax.dot_general(a, b, (((0,), (0,)), ((), ())), preferred_element_type=F32)


def _sigmoid(x):
    return jax.nn.sigmoid(x)


def _inproj_kernel(x_ref, g_ref, w_ref, o_ref, h_scr):
    @pl.when(pl.program_id(1) == 0)
    def _():
        x = x_ref[...]
        y = x * lax.rsqrt(jnp.mean(x * x, axis=-1, keepdims=True) + EPS) * g_ref[...]
        h_scr[...] = y.astype(BF16)

    o_ref[...] = _dot(h_scr[...], w_ref[...])


def _inproj(x2, gain, w_pad, tm, tn):
    s, d = x2.shape
    n = w_pad.shape[1]
    return pl.pallas_call(
        _inproj_kernel,
        out_shape=jax.ShapeDtypeStruct((s, n), F32),
        grid=(s // tm, n // tn),
        in_specs=[
            pl.BlockSpec((tm, d), lambda i, j: (i, 0)),
            pl.BlockSpec((1, d), lambda i, j: (0, 0)),
            pl.BlockSpec((d, tn), lambda i, j: (0, j)),
        ],
        out_specs=pl.BlockSpec((tm, tn), lambda i, j: (i, j)),
        scratch_shapes=[pltpu.VMEM((tm, d), BF16)],
        compiler_params=_cparams(("arbitrary", "arbitrary")),
        name="inproj",
    )(x2, gain, w_pad)


def _hgrn_kernel(q_ref, f_ref, i_ref, g_ref, lbp_ref, on_ref, o_ref,
                 st_ref, b_scr, k_scr, qs_scr, att_scr, *, nch):
    c_ = CHUNK
    half = c_ // 2

    @pl.when(pl.program_id(1) == 0)
    def _():
        st_ref[...] = jnp.zeros_like(st_ref)

    l0 = lbp_ref[0:1, :]
    l1 = lbp_ref[1:2, :]
    lm = jnp.maximum(l0, l1)
    e0 = jnp.exp(l0 - lm)
    e1 = jnp.exp(l1 - lm)
    lb = e0 / (e0 + e1)

    row = lax.broadcasted_iota(I32, (c_, c_), 0)
    col = lax.broadcasted_iota(I32, (c_, c_), 1)
    tri = row >= col
    tri_bf = jnp.where(tri, 1.0, 0.0).astype(BF16)

    q = q_ref[...]
    qs_scr[...] = q * _sigmoid(q)
    fg = lb + (1.0 - lb) * _sigmoid(f_ref[...])
    k_scr[...] = 1.0 - fg
    lg = jnp.log(fg)

    d = A_HEAD_DIM
    hpg = q_ref.shape[1] // d
    heads = [slice(hh * d, (hh + 1) * d) for hh in range(hpg)]

    dev = jnp.zeros((1, 1), F32)
    for c in range(nch):
        l = lg[c * c_:(c + 1) * c_]
        hi = l.astype(BF16)
        r1 = l - hi.astype(F32)
        mid = r1.astype(BF16)
        lo = (r1 - mid.astype(F32)).astype(BF16)
        b = _dot(tri_bf, hi) + _dot(tri_bf, mid) + _dot(tri_bf, lo)
        b_scr[c * c_:(c + 1) * c_, :] = b
        rho = b[half - 1:half, :]
        dev = jnp.maximum(dev, jnp.max(jnp.abs(b - rho), keepdims=True))
    safe = dev[0, 0] <= EXP_SAFE

    @pl.when(safe)
    def _():
        for c in range(nch):
            sl = slice(c * c_, (c + 1) * c_)
            b = b_scr[sl, :]
            rho = b[half - 1:half, :]
            qt = (qs_scr[sl, :] * jnp.exp(b - rho)).astype(BF16)
            kt = (k_scr[sl, :] * jnp.exp(rho - b)).astype(BF16)
            for hh, hs in enumerate(heads):
                att_scr[hh, c] = jnp.where(tri, _dot_nt(qt[:, hs], kt[:, hs]), 0.0)

    @pl.when(jnp.logical_not(safe))
    def _():
        trow = lax.broadcasted_iota(I32, (c_, d), 0)
        for c in range(nch):
            sl = slice(c * c_, (c + 1) * c_)
            for hh, hs in enumerate(heads):
                b = b_scr[sl, hs]
                qv = qs_scr[sl, hs]

                def body(s, att):
                    bs = b_scr[pl.ds(c * c_ + s, 1), :][:, hs]
                    ks = k_scr[pl.ds(c * c_ + s, 1), :][:, hs]
                    dd = jnp.where(trow >= s, b - bs, -jnp.inf)
                    colv = jnp.sum(qv * ks * jnp.exp(dd), axis=-1, keepdims=True)
                    return jnp.where(col == s, colv, att)

                att_scr[hh, c] = lax.fori_loop(0, c_, body, jnp.zeros((c_, c_), F32))

    on = on_ref[...]
    for c in range(nch):
        sl = slice(c * c_, (c + 1) * c_)
        b = b_scr[sl, :]
        bend = b[c_ - 1:c_, :]
        v = i_ref[sl, :].astype(BF16)
        qe = (qs_scr[sl, :] * jnp.exp(b)).astype(BF16)
        kd = (k_scr[sl, :] * jnp.exp(bend - b)).astype(BF16)
        decay = jnp.exp(bend)
        g = g_ref[sl, :]
        gate = g * _sigmoid(g)
        for hh, hs in enumerate(heads):
            st = st_ref[hh]
            o = _dot_nt(qe[:, hs], st.astype(BF16)) + _dot(att_scr[hh, c].astype(BF16), v[:, hs])
            st_ref[hh] = st * decay[:, hs] + _dot_tn(v[:, hs], kd[:, hs])
            o = o * lax.rsqrt(jnp.mean(o * o, axis=-1, keepdims=True) + EPS) * on
            o_ref[sl, hs] = (o * gate[:, hs]).astype(o_ref.dtype)


def _hgrn(proj, lower_bounds, out_norm, t_rows, hpg):
    s = proj.shape[0]
    nch = t_rows // CHUNK
    d = A_HEAD_DIM
    w = hpg * d
    nb = COL_QFIG // w

    def colspec(off):
        return pl.BlockSpec((t_rows, w), lambda h, i, off=off: (i, nb + off * (A_HEADS // hpg) + h))

    return pl.pallas_call(
        functools.partial(_hgrn_kernel, nch=nch),
        out_shape=jax.ShapeDtypeStruct((s, A_WIDTH), BF16),
        grid=(A_HEADS // hpg, s // t_rows),
        in_specs=[
            colspec(0), colspec(1), colspec(2), colspec(3),
            pl.BlockSpec((2, w), lambda h, i: (0, h)),
            pl.BlockSpec((1, d), lambda h, i: (0, 0)),
        ],
        out_specs=pl.BlockSpec((t_rows, w), lambda h, i: (i, h)),
        scratch_shapes=[
            pltpu.VMEM((hpg, d, d), F32),
            pltpu.VMEM((t_rows, w), F32),
            pltpu.VMEM((t_rows, w), F32),
            pltpu.VMEM((t_rows, w), F32),
            pltpu.VMEM((hpg, nch, CHUNK, CHUNK), F32),
        ],
        compiler_params=_cparams(("arbitrary", "arbitrary")),
        name="hgrn",
    )(proj, proj, proj, proj, lower_bounds, out_norm)


def _mla_prep_kernel(p_ref, qn_ref, kn_ref, wuq_ref, wukt_ref, wiq_ref,
                     qlat_ref, qidxt_ref, widxt_ref, ckvx_ref, ckvt_ref, kidx_ref):
    p = p_ref[...]
    cq = p[:, 0:Q_RANK]
    cq = cq * lax.rsqrt(jnp.mean(cq * cq, axis=-1, keepdims=True) + EPS) * qn_ref[...]
    cqb = cq.astype(BF16)
    ckv = p[:, Q_RANK:Q_RANK + KV_RANK]
    ckv = ckv * lax.rsqrt(jnp.mean(ckv * ckv, axis=-1, keepdims=True) + EPS) * kn_ref[...]

    q = _dot(cqb, wuq_ref[...])
    scale = B_QK_DIM ** -0.5 * LOG2E
    for h in range(B_HEADS):
        qh = q[:, h * B_QK_DIM:(h + 1) * B_QK_DIM].astype(BF16)
        qlat_ref[:, h * KV_RANK:(h + 1) * KV_RANK] = (_dot(qh, wukt_ref[h]) * scale).astype(BF16)
    qidxt_ref[...] = _dot(cqb, wiq_ref[...]).T.astype(BF16)

    rest = p[:, Q_RANK + KV_RANK:]
    widxt_ref[...] = rest.T[IDX_DIM:IDX_DIM + IDX_HEADS, :] * ((IDX_HEADS * IDX_DIM) ** -0.5)
    ckvx_ref[:, 0:KV_RANK] = ckv.astype(BF16)
    ckvx_ref[:, KV_RANK:] = jnp.ones((p.shape[0], KV_RANK), BF16)
    ckvt_ref[...] = ckv.T.astype(BF16)
    kidx_ref[...] = rest[:, 0:IDX_DIM].astype(BF16)


def _mla_prep(proj, q_norm, kv_norm, w_uq, w_ukt, w_iq, tm):
    s = proj.shape[0]
    return pl.pallas_call(
        _mla_prep_kernel,
        out_shape=[
            jax.ShapeDtypeStruct((s, B_HEADS * KV_RANK), BF16),
            jax.ShapeDtypeStruct((IDX_HEADS * IDX_DIM, s), BF16),
            jax.ShapeDtypeStruct((IDX_HEADS, s), F32),
            jax.ShapeDtypeStruct((s, 2 * KV_RANK), BF16),
            jax.ShapeDtypeStruct((KV_RANK, s), BF16),
            jax.ShapeDtypeStruct((s, IDX_DIM), BF16),
        ],
        grid=(s // tm,),
        in_specs=[
            pl.BlockSpec((tm, SMALL_W), lambda i: (i, COL_SMALL // SMALL_W)),
            _const_spec((1, Q_RANK)),
            _const_spec((1, KV_RANK)),
            _const_spec((Q_RANK, B_HEADS * B_QK_DIM)),
            _const_spec((B_HEADS, B_QK_DIM, KV_RANK)),
            _const_spec((Q_RANK, IDX_HEADS * IDX_DIM)),
        ],
        out_specs=[
            pl.BlockSpec((tm, B_HEADS * KV_RANK), lambda i: (i, 0)),
            pl.BlockSpec((IDX_HEADS * IDX_DIM, tm), lambda i: (0, i)),
            pl.BlockSpec((IDX_HEADS, tm), lambda i: (0, i)),
            pl.BlockSpec((tm, 2 * KV_RANK), lambda i: (i, 0)),
            pl.BlockSpec((KV_RANK, tm), lambda i: (0, i)),
            pl.BlockSpec((tm, IDX_DIM), lambda i: (i, 0)),
        ],
        compiler_params=_cparams(("arbitrary",)),
        name="mla_prep",
    )(proj, q_norm, kv_norm, w_uq, w_ukt, w_iq)


def _bias_tab_kernel(bkt_ref, rb_ref, o_ref):
    far = N_BUCKETS // 2 - 1
    for v in range(2):
        bkt = bkt_ref[v]
        for h in range(B_HEADS):
            acc = jnp.zeros(bkt.shape, F32)
            for k in range(N_BUCKETS):
                acc = jnp.where(bkt == k, (rb_ref[k, h] - rb_ref[far, h]) * LOG2E, acc)
            o_ref[v, h] = acc


def _bias_tab(buckets, rel_bias):
    return pl.pallas_call(
        _bias_tab_kernel,
        out_shape=jax.ShapeDtypeStruct((2, B_HEADS, TQ, TKN), F32),
        in_specs=[
            pl.BlockSpec(memory_space=pltpu.VMEM),
            pl.BlockSpec(memory_space=pltpu.SMEM),
        ],
        out_specs=pl.BlockSpec(memory_space=pltpu.VMEM),
        compiler_params=pltpu.CompilerParams(vmem_limit_bytes=VMEM_LIMIT),
        name="bias_tab",
    )(buckets, rel_bias)


def _t5_bucket(rel):
    half = N_BUCKETS // 2
    max_exact = half // 2
    ret = jnp.where(rel > 0, half, 0)
    n = jnp.abs(rel)
    nf = jnp.maximum(n, 1).astype(jnp.float32)
    large = max_exact + (jnp.log(nf / max_exact) / math.log(MAX_DISTANCE / max_exact) * (half - max_exact)).astype(jnp.int32)
    large = jnp.minimum(large, half - 1)
    return ret + jnp.where(n < max_exact, n, large)


def _dsa_kernel(qlat_ref, qidxt_ref, widxt_ref, kidx_ref, ckvt_ref, ckvx_ref, biasd_ref, wuv_ref,
                o_ref, isc_scr, t_scr, lhs_scr, rhs_scr, m_scr, acc_scr, p_scr, al_scr, qp_scr,
                *, top_k):
    qi = pl.program_id(0)
    q0 = qi * TQ
    kf = float(top_k)
    qlane = lax.broadcasted_iota(I32, (1, TQ), 1)
    lim = q0 + (qlane // CHUNK + 1) * CHUNK
    krow = lax.broadcasted_iota(I32, (TKA, TQ), 0)

    def fold(x, op):
        n = x.shape[0] // 8
        x = x.reshape(n, 8, TQ)
        while n > 1:
            n //= 2
            x = op(x[:n], x[n:2 * n])
        return x[0]

    def lanes_x(v, width):
        return jnp.concatenate([v] * (width // LANES), axis=1)

    na = (q0 + TQ + TKA - 1) // TKA
    qt = qidxt_ref[...]
    wt = widxt_ref[...]
    wp = []
    for p in range(IDX_HEADS // 2):
        qa = qt[(2 * p) * IDX_DIM:(2 * p + 1) * IDX_DIM, :]
        qb = qt[(2 * p + 1) * IDX_DIM:(2 * p + 2) * IDX_DIM, :]
        qp_scr[p] = jnp.concatenate([qa, qb], axis=1)
        wp.append(jnp.concatenate([wt[2 * p:2 * p + 1, :], wt[2 * p + 1:2 * p + 2, :]], axis=1))

    def scores(k0):
        kb = kidx_ref[pl.ds(k0, TKA), :]
        acc = jnp.zeros((TKA, TQ), F32)
        for p in range(IDX_HEADS // 2):
            s2 = jnp.maximum(_dot(kb, qp_scr[p]), 0.0) * wp[p]
            acc = acc + (s2[:, 0:TQ] + s2[:, TQ:])
        return acc

    def emit(k0, acc, stats):
        mn, mx, c_ge0, c_gt0 = stats
        mn = jnp.minimum(mn, fold(acc, jnp.minimum))
        mx = jnp.maximum(mx, fold(acc, jnp.maximum))
        a = jnp.where((k0 + krow) < lim, acc, -jnp.inf)
        c_ge0 = c_ge0 + fold(jnp.where(a >= 0.0, 1.0, 0.0), jnp.add)
        c_gt0 = c_gt0 + fold(jnp.where(a > 0.0, 1.0, 0.0), jnp.add)
        isc_scr[pl.ds(k0, TKA), :] = a
        return mn, mx, c_ge0, c_gt0

    def phase_a(jb, stats):
        k0 = pl.multiple_of(jb * (2 * TKA), 2 * TKA)
        a0 = scores(k0)
        a1 = scores(k0 + TKA)
        return emit(k0 + TKA, a1, emit(k0, a0, stats))

    zeros8 = jnp.zeros((8, TQ), F32)
    mn, mx, pc_ge0, pc_gt0 = lax.fori_loop(
        0, (na + 1) // 2, phase_a,
        (jnp.full((8, TQ), jnp.inf, F32), jnp.full((8, TQ), -jnp.inf, F32), zeros8, zeros8))

    def count(pred):
        def body(g, cnt):
            k0 = pl.multiple_of(g * TKA, TKA)
            x = isc_scr[pl.ds(k0, TKA), :]
            return cnt + fold(jnp.where(pred(x, k0 + krow), 1.0, 0.0), jnp.add)
        return jnp.sum(lax.fori_loop(0, na, body, zeros8), axis=0, keepdims=True)

    t_scr[...] = jnp.full(t_scr.shape, -jnp.inf, F32)

    @pl.when(q0 + CHUNK > top_k)
    def _():
        def row_total(part):
            return jnp.sum(part, axis=0, keepdims=True)

        rmin = jnp.min(mn, axis=0, keepdims=True)
        rmax = jnp.max(mx, axis=0, keepdims=True)
        n_adm = lim.astype(F32)
        c_ge0, c_gt0 = row_total(pc_ge0), row_total(pc_gt0)
        above = c_gt0 >= kf
        below = c_ge0 < kf
        tiny = float(jnp.finfo(F32).tiny)
        hi_top = rmax + (rmax - rmin) + jnp.abs(rmax) * 1e-3 + 1e-30
        lo0 = jnp.where(below, rmin, 0.0)
        clo0 = jnp.where(below, n_adm, c_ge0)
        hi0 = jnp.where(below, 0.0, jnp.where(above, hi_top, tiny))
        chi0 = jnp.where(below, c_ge0, jnp.where(above, 0.0, c_gt0))
        log_k = math.log(kf)

        def plan(it, lo, hi, clo, chi, done):
            bis = 0.5 * lo + 0.5 * hi
            act = jnp.where(jnp.logical_and(done == 0.0, jnp.logical_and(bis > lo, bis < hi)), 1.0, 0.0)
            llo = jnp.log(clo)
            lhi = jnp.log(jnp.maximum(chi, 0.5))
            t = jnp.clip((llo - log_k) * pl.reciprocal(llo - lhi, approx=True), 0.03, 0.97)
            guess = lo + t * (hi - lo)
            use = jnp.logical_and(jnp.logical_and(guess > lo, guess < hi), it % 3 != 2)
            return act, jnp.sum(act), jnp.where(use, guess, bis)

        def count_ge(mid):
            def body(g, cnt):
                k0 = pl.multiple_of(g * TKA, TKA)
                return cnt + fold(jnp.where(isc_scr[pl.ds(k0, TKA), :] >= mid, 1.0, 0.0), jnp.add)

            return row_total(lax.fori_loop(0, na, body, zeros8))

        def cond(st):
            it, nact = st[0], st[1]
            return jnp.logical_and(it < MAX_BISECT, nact > 0.0)

        def body(st):
            it, nact, lo, hi, clo, chi, done, act, mid = st
            c = count_ge(mid)
            on = act > 0.0
            up = jnp.logical_and(on, c >= kf)
            dn = jnp.logical_and(on, c < kf)
            lo = jnp.where(up, mid, lo)
            clo = jnp.where(up, c, clo)
            hi = jnp.where(dn, mid, hi)
            chi = jnp.where(dn, c, chi)
            done = jnp.where(jnp.logical_and(on, c == kf), 1.0, done)
            act, nact, mid = plan(it + 1, lo, hi, clo, chi, done)
            return it + 1, nact, lo, hi, clo, chi, done, act, mid

        done0 = jnp.where(c_ge0 == kf, 1.0, 0.0)
        act0, nact0, mid0 = plan(0, lo0, hi0, clo0, chi0, done0)
        st = lax.while_loop(cond, body, (jnp.int32(0), nact0, lo0, hi0, clo0, chi0, done0, act0, mid0))
        t_lo, t_hi, done = st[2], st[3], st[6]
        t_scr[0:1, :] = t_lo

        n_tie = jnp.sum(jnp.where(done == 0.0, 1.0, 0.0))

        @pl.when(n_tie > 0.0)
        def _():
            lo, hi, tie = t_lo, t_hi, done == 0.0
            r = kf - count(lambda x, idx: x >= hi)
            p = jnp.zeros((1, TQ), I32)
            nbits = max(1, (isc_scr.shape[0] - 1).bit_length())
            for bit in range(nbits - 1, -1, -1):
                cand = p + (1 << bit)
                f = count(lambda x, idx: jnp.logical_and(jnp.logical_and(x >= lo, x < hi), idx < cand))
                p = jnp.where(f < r, cand, p)

            def drop(g, carry):
                k0 = pl.multiple_of(g * TKA, TKA)
                x = isc_scr[pl.ds(k0, TKA), :]
                ex = jnp.logical_and(jnp.logical_and(tie, (k0 + krow) > p),
                                     jnp.logical_and(x >= lo, x < hi))
                isc_scr[pl.ds(k0, TKA), :] = jnp.where(ex, -jnp.inf, x)
                return carry

            lax.fori_loop(0, na, drop, 0)

    thr = t_scr[0:1, :]
    for h in range(B_HEADS):
        lhs_scr[h * TQ:(h + 1) * TQ, 0:KV_RANK] = qlat_ref[:, h * KV_RANK:(h + 1) * KV_RANK]

    @pl.when(qi == 0)
    def _():
        eye = jnp.where(lax.broadcasted_iota(I32, (TQ, TQ), 0) == lax.broadcasted_iota(I32, (TQ, TQ), 1),
                        1.0, 0.0).astype(BF16)
        for h in range(B_HEADS):
            lhs_scr[h * TQ:(h + 1) * TQ, KV_RANK:] = eye

    m_scr[...] = jnp.full(m_scr.shape, NEG, F32)
    acc_scr[...] = jnp.zeros_like(acc_scr)

    def accumulate(h, k0_prev, w_prev):
        rs = slice(h * TQ, (h + 1) * TQ)
        kv = ckvx_ref[pl.ds(k0_prev, w_prev), :]
        acc_scr[rs, :] = (acc_scr[rs, :] * lanes_x(al_scr[rs, :], 2 * KV_RANK)
                          + _dot(p_scr[rs, 0:w_prev], kv))

    def stage(k0_prev, w_prev, k0, width, key_lim, with_bias):
        x = isc_scr[pl.ds(k0, width), :]
        sel = jnp.logical_and(x >= thr, (k0 + krow[0:width, :]) < key_lim)
        rhs_scr[0:KV_RANK, 0:width] = ckvt_ref[:, pl.ds(k0, width)]
        rhs_scr[KV_RANK:, 0:width] = jnp.where(sel, 0.0, NEG).T.astype(BF16)
        rhs = rhs_scr[:, 0:width]
        for h in range(B_HEADS):
            rs = slice(h * TQ, (h + 1) * TQ)
            if w_prev:
                accumulate(h, k0_prev, w_prev)
            s = _dot(lhs_scr[rs, :], rhs)
            if with_bias:
                s = s + biasd_ref[0, h]
            m_old = m_scr[rs, :]
            m_new = jnp.maximum(m_old, jnp.max(s, axis=1, keepdims=True))
            al_scr[rs, :] = jnp.exp2(m_old - m_new)
            p_scr[rs, 0:width] = jnp.exp2(s - lanes_x(m_new, width)).astype(BF16)
            m_scr[rs, :] = m_new

    far_lim = jnp.maximum(q0 - TQ, 0)
    nf = (far_lim + TK - 1) // TK

    @pl.when(nf == 0)
    def _():
        p_scr[...] = jnp.zeros_like(p_scr)
        al_scr[...] = jnp.ones_like(al_scr)

    @pl.when(nf > 0)
    def _():
        stage(0, 0, 0, TK, far_lim, False)

    def far_block(jb, carry):
        stage(pl.multiple_of((jb - 1) * TK, TK), TK, pl.multiple_of(jb * TK, TK), TK, far_lim, False)
        return carry

    lax.fori_loop(1, nf, far_block, 0)
    k0_last = pl.multiple_of(jnp.maximum(nf - 1, 0) * TK, TK)
    k0_near = pl.multiple_of(far_lim, TQ)
    stage(k0_last, TK, k0_near, TKN, lim, True)
    for h in range(B_HEADS):
        accumulate(h, k0_near, TKN)

    for h in range(B_HEADS):
        rs = slice(h * TQ, (h + 1) * TQ)
        a = acc_scr[rs, :]
        olat = (a[:, 0:KV_RANK] / a[:, KV_RANK:]).astype(BF16)
        o_ref[:, h * B_V_DIM:(h + 1) * B_V_DIM] = _dot(olat, wuv_ref[h]).astype(o_ref.dtype)


def _dsa(qlat, qidxt, widxt, kidx, ckvt, ckvx, biasd, w_uv, top_k):
    s = qlat.shape[0]
    assert s % (2 * TKA) == 0 and TK == TKA and top_k % TQ == 0
    return pl.pallas_call(
        functools.partial(_dsa_kernel, top_k=top_k),
        out_shape=jax.ShapeDtypeStruct((s, B_WIDTH), BF16),
        grid=(s // TQ,),
        in_specs=[
            pl.BlockSpec((TQ, B_HEADS * KV_RANK), lambda i: (i, 0)),
            pl.BlockSpec((IDX_HEADS * IDX_DIM, TQ), lambda i: (0, i)),
            pl.BlockSpec((IDX_HEADS, TQ), lambda i: (0, i)),
            _const_spec((s, IDX_DIM)),
            _const_spec((KV_RANK, s)),
            _const_spec((s, 2 * KV_RANK)),
            pl.BlockSpec((1, B_HEADS, TQ, TKN), lambda i: (jnp.minimum(i, 1), 0, 0, 0)),
            _const_spec((B_HEADS, KV_RANK, B_V_DIM)),
        ],
        out_specs=pl.BlockSpec((TQ, B_WIDTH), lambda i: (i, 0)),
        scratch_shapes=[
            pltpu.VMEM((s, TQ), F32),
            pltpu.VMEM((8, TQ), F32),
            pltpu.VMEM((B_HEADS * TQ, 2 * KV_RANK), BF16),
            pltpu.VMEM((2 * KV_RANK, TK), BF16),
            pltpu.VMEM((B_HEADS * TQ, LANES), F32),
            pltpu.VMEM((B_HEADS * TQ, 2 * KV_RANK), F32),
            pltpu.VMEM((B_HEADS * TQ, TK), BF16),
            pltpu.VMEM((B_HEADS * TQ, LANES), F32),
            pltpu.VMEM((IDX_HEADS // 2, IDX_DIM, 2 * TQ), BF16),
        ],
        compiler_params=_cparams(("arbitrary",)),
        name="dsa",
    )(qlat, qidxt, widxt, kidx, ckvt, ckvx, biasd, w_uv)


def _merge_kernel(x_ref, hg_ref, ob_ref, ga_ref, gb_ref, wa_ref, wb_ref, wo_ref, fn_ref,
                  x1_ref, h2_ref):
    ya = _dot(hg_ref[...], wa_ref[...])
    yb = _dot(ob_ref[...], wb_ref[...])
    mix = _sigmoid(ga_ref[...]) * ya + _sigmoid(gb_ref[...]) * yb
    x1 = x_ref[...] + _dot(mix.astype(BF16), wo_ref[...])
    x1_ref[...] = x1
    h2 = x1 * lax.rsqrt(jnp.mean(x1 * x1, axis=-1, keepdims=True) + EPS) * fn_ref[...]
    h2_ref[...] = h2.astype(BF16)


def _merge(x2, hg, ob, proj, w_a, w_b, w_o, ffn_norm, tm):
    s, d = x2.shape
    gblk = COL_GATE // d
    row = lambda i: (i, 0)
    return pl.pallas_call(
        _merge_kernel,
        out_shape=[jax.ShapeDtypeStruct((s, d), F32), jax.ShapeDtypeStruct((s, d), BF16)],
        grid=(s // tm,),
        in_specs=[
            pl.BlockSpec((tm, d), row),
            pl.BlockSpec((tm, A_WIDTH), row),
            pl.BlockSpec((tm, B_WIDTH), row),
            pl.BlockSpec((tm, d), lambda i: (i, gblk)),
            pl.BlockSpec((tm, d), lambda i: (i, gblk + 1)),
            _const_spec((A_WIDTH, d)),
            _const_spec((B_WIDTH, d)),
            _const_spec((d, d)),
            _const_spec((1, d)),
        ],
        out_specs=[pl.BlockSpec((tm, d), row), pl.BlockSpec((tm, d), row)],
        compiler_params=_cparams(("arbitrary",)),
        name="merge",
    )(x2, hg, ob, proj, proj, w_a, w_b, w_o, ffn_norm)


def _ffn_kernel(x1_ref, h2_ref, wup_ref, cw_ref, cb_ref, wdn_ref, fin_ref, o_ref,
                tail_scr, u_scr, act_scr, *, cbw):
    tm = h2_ref.shape[0]
    dff = wdn_ref.shape[0]
    halo = 8

    @pl.when(pl.program_id(0) == 0)
    def _():
        tail_scr[...] = jnp.zeros_like(tail_scr)

    h2 = h2_ref[...]

    def conv_cols(c0):
        u = _dot(h2, wup_ref[:, c0:c0 + cbw])
        u_scr[0:halo, :] = tail_scr[:, c0:c0 + cbw]
        u_scr[halo:, :] = u
        tail_scr[:, c0:c0 + cbw] = u[tm - halo:, :]
        cw = cw_ref[:, c0:c0 + cbw]
        return (cw[2:3] * u + cw[1:2] * u_scr[halo - 1:halo - 1 + tm, :]
                + cw[0:1] * u_scr[halo - 2:halo - 2 + tm, :] + cb_ref[:, c0:c0 + cbw])

    for j in range(dff // cbw):
        a = conv_cols(j * cbw)
        v = conv_cols(dff + j * cbw)
        act_scr[:, j * cbw:(j + 1) * cbw] = (a * _sigmoid(a) * v).astype(BF16)

    y = x1_ref[...] + _dot(act_scr[...], wdn_ref[...])
    o_ref[...] = y * lax.rsqrt(jnp.mean(y * y, axis=-1, keepdims=True) + EPS) * fin_ref[...]


def _ffn(x1, h2, w_up, conv_w, conv_b, w_down, final_norm, tm, cbw):
    s, d = x1.shape
    dff = w_down.shape[0]
    row = lambda i: (i, 0)
    return pl.pallas_call(
        functools.partial(_ffn_kernel, cbw=cbw),
        out_shape=jax.ShapeDtypeStruct((s, d), F32),
        grid=(s // tm,),
        in_specs=[
            pl.BlockSpec((tm, d), row),
            pl.BlockSpec((tm, d), row),
            _const_spec((d, 2 * dff)),
            _const_spec((CONV_WIDTH, 2 * dff)),
            _const_spec((1, 2 * dff)),
            _const_spec((dff, d)),
            _const_spec((1, d)),
        ],
        out_specs=pl.BlockSpec((tm, d), row),
        scratch_shapes=[
            pltpu.VMEM((8, 2 * dff), F32),
            pltpu.VMEM((tm + 8, cbw), F32),
            pltpu.VMEM((tm, dff), BF16),
        ],
        compiler_params=_cparams(("arbitrary",)),
        name="ffn",
    )(x1, h2, w_up, conv_w, conv_b, w_down, final_norm)


def _row_tile(s, want):
    t = min(want, s)
    while s % t:
        t //= 2
    return t


def kernel(x, attn_norm, w_in, lower_bounds, hgrn_out_norm, q_norm, kv_norm, w_uq, w_uk, w_uv, w_iq,
           rel_bias, w_branch_a, w_branch_b, w_o, ffn_norm, w_up, conv_w, conv_b, w_down, final_norm):
    bsz, s, d = x.shape
    assert bsz == 1 and attn_norm.shape[0] == 1 and lower_bounds.shape[0] == 2
    x2 = x.reshape(s, d)
    top_k = min(TOPK_MAX, s // 4)

    wi = w_in[0]
    n_small = Q_RANK + KV_RANK + IDX_DIM + IDX_HEADS
    w_pad = jnp.concatenate(
        [wi[:, :COL_GATE], wi[:, COL_GATE + n_small:], wi[:, COL_GATE:COL_GATE + n_small],
         jnp.zeros((d, SMALL_W - n_small), wi.dtype)], axis=1).astype(BF16)

    proj = _inproj(x2, attn_norm, w_pad, _row_tile(s, 1024), N_PROJ // 4)

    hg = _hgrn(proj, lower_bounds, hgrn_out_norm, _row_tile(s, 512), 4)

    qlat, qidxt, widxt, ckvx, ckvt, kidx = _mla_prep(
        proj, q_norm, kv_norm, w_uq[0].astype(BF16), jnp.swapaxes(w_uk[0], 1, 2).astype(BF16),
        w_iq[0].astype(BF16), _row_tile(s, 512))

    t_loc = jnp.arange(TQ, dtype=jnp.int32)[None, :, None]
    s_loc = jnp.arange(TKN, dtype=jnp.int32)[None, None, :]
    off = jnp.array([0, -TQ], jnp.int32)[:, None, None]
    buckets = _t5_bucket(s_loc - t_loc + off)
    biasd = _bias_tab(buckets, rel_bias)

    ob = _dsa(qlat, qidxt, widxt, kidx, ckvt, ckvx, biasd, w_uv[0].astype(BF16), top_k)

    x1, h2 = _merge(x2, hg, ob, proj, w_branch_a[0].astype(BF16), w_branch_b[0].astype(BF16),
                    w_o[0].astype(BF16), ffn_norm, _row_tile(s, 512))

    out = _ffn(x1, h2, w_up[0].astype(BF16), conv_w[0], conv_b, w_down[0].astype(BF16),
               final_norm.reshape(1, d), _row_tile(s, 512), 256)
    return out.reshape(bsz, s, d)
```

```python
import functools
import math

import jax
import jax.numpy as jnp
from jax import lax
from jax.experimental import pallas as pl
from jax.experimental.pallas import tpu as pltpu

F32 = jnp.float32
BF16 = jnp.bfloat16
I32 = jnp.int32

EPS = 1e-6
CHUNK = 64
A_HEADS = 8
A_HEAD_DIM = 128
A_WIDTH = A_HEADS * A_HEAD_DIM
B_HEADS = 16
B_QK_DIM = 64
B_V_DIM = 64
B_WIDTH = B_HEADS * B_V_DIM
Q_RANK = 256
KV_RANK = 128
IDX_HEADS = 8
IDX_DIM = 64
TOPK_MAX = 256
N_BUCKETS = 32
MAX_DISTANCE = 128
CONV_WIDTH = 3

LANES = 128
VMEM_LIMIT = 56 * 1024 * 1024

COL_QFIG = 0
COL_GATE = 4 * A_WIDTH
COL_SMALL = COL_GATE + 2048
SMALL_W = 512
N_PROJ = COL_SMALL + SMALL_W

TQ = 128
TKA = 512
TK = 512
TKN = 2 * TQ
NEG = -1e30
EXP_SAFE = 80.0
MAX_BISECT = 1024
LOG2E = 1.4426950408889634


def _cparams(sem):
    return pltpu.CompilerParams(dimension_semantics=sem, vmem_limit_bytes=VMEM_LIMIT)


def _const_spec(shape):
    nd = len(shape)
    return pl.BlockSpec(shape, lambda *_: (0,) * nd, pipeline_mode=pl.Buffered(1))


def _dot(a, b):
    return jnp.dot(a, b, preferred_element_type=F32)


def _dot_nt(a, b):
    return lax.dot_general(a, b, (((1,), (1,)), ((), ())), preferred_element_type=F32)


def _dot_tn(a, b):
    return lax.dot_general(a, b, (((0,), (0,)), ((), ())), preferred_element_type=F32)


def _sigmoid(x):
    return jax.nn.sigmoid(x)


def _inproj_kernel(x_ref, g_ref, w_ref, o_ref, h_scr):
    @pl.when(pl.program_id(1) == 0)
    def _():
        x = x_ref[...]
        y = x * lax.rsqrt(jnp.mean(x * x, axis=-1, keepdims=True) + EPS) * g_ref[...]
        h_scr[...] = y.astype(BF16)

    o_ref[...] = _dot(h_scr[...], w_ref[...])


def _inproj(x2, gain, w_pad, tm, tn):
    s, d = x2.shape
    n = w_pad.shape[1]
    return pl.pallas_call(
        _inproj_kernel,
        out_shape=jax.ShapeDtypeStruct((s, n), F32),
        grid=(s // tm, n // tn),
        in_specs=[
            pl.BlockSpec((tm, d), lambda i, j: (i, 0)),
            pl.BlockSpec((1, d), lambda i, j: (0, 0)),
            pl.BlockSpec((d, tn), lambda i, j: (0, j)),
        ],
        out_specs=pl.BlockSpec((tm, tn), lambda i, j: (i, j)),
        scratch_shapes=[pltpu.VMEM((tm, d), BF16)],
        compiler_params=_cparams(("arbitrary", "arbitrary")),
        name="inproj",
    )(x2, gain, w_pad)


def _hgrn_kernel(q_ref, f_ref, i_ref, g_ref, lbp_ref, on_ref, o_ref,
                 st_ref, b_scr, k_scr, qs_scr, att_scr, *, nch):
    c_ = CHUNK
    half = c_ // 2

    @pl.when(pl.program_id(1) == 0)
    def _():
        st_ref[...] = jnp.zeros_like(st_ref)

    l0 = lbp_ref[0:1, :]
    l1 = lbp_ref[1:2, :]
    lm = jnp.maximum(l0, l1)
    e0 = jnp.exp(l0 - lm)
    e1 = jnp.exp(l1 - lm)
    lb = e0 / (e0 + e1)

    row = lax.broadcasted_iota(I32, (c_, c_), 0)
    col = lax.broadcasted_iota(I32, (c_, c_), 1)
    tri = row >= col
    tri_bf = jnp.where(tri, 1.0, 0.0).astype(BF16)

    q = q_ref[...]
    qs_scr[...] = q * _sigmoid(q)
    fg = lb + (1.0 - lb) * _sigmoid(f_ref[...])
    k_scr[...] = 1.0 - fg
    lg = jnp.log(fg)

    d = A_HEAD_DIM
    hpg = q_ref.shape[1] // d
    heads = [slice(hh * d, (hh + 1) * d) for hh in range(hpg)]

    dev = jnp.zeros((1, 1), F32)
    for c in range(nch):
        l = lg[c * c_:(c + 1) * c_]
        hi = l.astype(BF16)
        r1 = l - hi.astype(F32)
        mid = r1.astype(BF16)
        lo = (r1 - mid.astype(F32)).astype(BF16)
        b = _dot(tri_bf, hi) + _dot(tri_bf, mid) + _dot(tri_bf, lo)
        b_scr[c * c_:(c + 1) * c_, :] = b
        rho = b[half - 1:half, :]
        dev = jnp.maximum(dev, jnp.max(jnp.abs(b - rho), keepdims=True))
    safe = dev[0, 0] <= EXP_SAFE

    @pl.when(safe)
    def _():
        for c in range(nch):
            sl = slice(c * c_, (c + 1) * c_)
            b = b_scr[sl, :]
            rho = b[half - 1:half, :]
            qt = (qs_scr[sl, :] * jnp.exp(b - rho)).astype(BF16)
            kt = (k_scr[sl, :] * jnp.exp(rho - b)).astype(BF16)
            for hh, hs in enumerate(heads):
                att_scr[hh, c] = jnp.where(tri, _dot_nt(qt[:, hs], kt[:, hs]), 0.0)

    @pl.when(jnp.logical_not(safe))
    def _():
        trow = lax.broadcasted_iota(I32, (c_, d), 0)
        for c in range(nch):
            sl = slice(c * c_, (c + 1) * c_)
            for hh, hs in enumerate(heads):
                b = b_scr[sl, hs]
                qv = qs_scr[sl, hs]

                def body(s, att):
                    bs = b_scr[pl.ds(c * c_ + s, 1), :][:, hs]
                    ks = k_scr[pl.ds(c * c_ + s, 1), :][:, hs]
                    dd = jnp.where(trow >= s, b - bs, -jnp.inf)
                    colv = jnp.sum(qv * ks * jnp.exp(dd), axis=-1, keepdims=True)
                    return jnp.where(col == s, colv, att)

                att_scr[hh, c] = lax.fori_loop(0, c_, body, jnp.zeros((c_, c_), F32))

    on = on_ref[...]
    for c in range(nch):
        sl = slice(c * c_, (c + 1) * c_)
        b = b_scr[sl, :]
        bend = b[c_ - 1:c_, :]
        v = i_ref[sl, :].astype(BF16)
        qe = (qs_scr[sl, :] * jnp.exp(b)).astype(BF16)
        kd = (k_scr[sl, :] * jnp.exp(bend - b)).astype(BF16)
        decay = jnp.exp(bend)
        g = g_ref[sl, :]
        gate = g * _sigmoid(g)
        for hh, hs in enumerate(heads):
            st = st_ref[hh]
            o = _dot_nt(qe[:, hs], st.astype(BF16)) + _dot(att_scr[hh, c].astype(BF16), v[:, hs])
            st_ref[hh] = st * decay[:, hs] + _dot_tn(v[:, hs], kd[:, hs])
            o = o * lax.rsqrt(jnp.mean(o * o, axis=-1, keepdims=True) + EPS) * on
            o_ref[sl, hs] = (o * gate[:, hs]).astype(o_ref.dtype)


def _hgrn(proj, lower_bounds, out_norm, t_rows, hpg):
    s = proj.shape[0]
    nch = t_rows // CHUNK
    d = A_HEAD_DIM
    w = hpg * d
    nb = COL_QFIG // w

    def colspec(off):
        return pl.BlockSpec((t_rows, w), lambda h, i, off=off: (i, nb + off * (A_HEADS // hpg) + h))

    return pl.pallas_call(
        functools.partial(_hgrn_kernel, nch=nch),
        out_shape=jax.ShapeDtypeStruct((s, A_WIDTH), BF16),
        grid=(A_HEADS // hpg, s // t_rows),
        in_specs=[
            colspec(0), colspec(1), colspec(2), colspec(3),
            pl.BlockSpec((2, w), lambda h, i: (0, h)),
            pl.BlockSpec((1, d), lambda h, i: (0, 0)),
        ],
        out_specs=pl.BlockSpec((t_rows, w), lambda h, i: (i, h)),
        scratch_shapes=[
            pltpu.VMEM((hpg, d, d), F32),
            pltpu.VMEM((t_rows, w), F32),
            pltpu.VMEM((t_rows, w), F32),
            pltpu.VMEM((t_rows, w), F32),
            pltpu.VMEM((hpg, nch, CHUNK, CHUNK), F32),
        ],
        compiler_params=_cparams(("arbitrary", "arbitrary")),
        name="hgrn",
    )(proj, proj, proj, proj, lower_bounds, out_norm)


def _mla_prep_kernel(p_ref, qn_ref, kn_ref, wuq_ref, wukt_ref, wiq_ref,
                     qlat_ref, qidxt_ref, widxt_ref, ckvx_ref, ckvt_ref, kidx_ref):
    p = p_ref[...]
    cq = p[:, 0:Q_RANK]
    cq = cq * lax.rsqrt(jnp.mean(cq * cq, axis=-1, keepdims=True) + EPS) * qn_ref[...]
    cqb = cq.astype(BF16)
    ckv = p[:, Q_RANK:Q_RANK + KV_RANK]
    ckv = ckv * lax.rsqrt(jnp.mean(ckv * ckv, axis=-1, keepdims=True) + EPS) * kn_ref[...]

    q = _dot(cqb, wuq_ref[...])
    scale = B_QK_DIM ** -0.5 * LOG2E
    for h in range(B_HEADS):
        qh = q[:, h * B_QK_DIM:(h + 1) * B_QK_DIM].astype(BF16)
        qlat_ref[:, h * KV_RANK:(h + 1) * KV_RANK] = (_dot(qh, wukt_ref[h]) * scale).astype(BF16)
    qidxt_ref[...] = _dot(cqb, wiq_ref[...]).T.astype(BF16)

    rest = p[:, Q_RANK + KV_RANK:]
    widxt_ref[...] = rest.T[IDX_DIM:IDX_DIM + IDX_HEADS, :] * ((IDX_HEADS * IDX_DIM) ** -0.5)
    ckvx_ref[:, 0:KV_RANK] = ckv.astype(BF16)
    ckvx_ref[:, KV_RANK:] = jnp.ones((p.shape[0], KV_RANK), BF16)
    ckvt_ref[...] = ckv.T.astype(BF16)
    kidx_ref[...] = rest[:, 0:IDX_DIM].astype(BF16)


def _mla_prep(proj, q_norm, kv_norm, w_uq, w_ukt, w_iq, tm):
    s = proj.shape[0]
    return pl.pallas_call(
        _mla_prep_kernel,
        out_shape=[
            jax.ShapeDtypeStruct((s, B_HEADS * KV_RANK), BF16),
            jax.ShapeDtypeStruct((IDX_HEADS * IDX_DIM, s), BF16),
            jax.ShapeDtypeStruct((IDX_HEADS, s), F32),
            jax.ShapeDtypeStruct((s, 2 * KV_RANK), BF16),
            jax.ShapeDtypeStruct((KV_RANK, s), BF16),
            jax.ShapeDtypeStruct((s, IDX_DIM), BF16),
        ],
        grid=(s // tm,),
        in_specs=[
            pl.BlockSpec((tm, SMALL_W), lambda i: (i, COL_SMALL // SMALL_W)),
            _const_spec((1, Q_RANK)),
            _const_spec((1, KV_RANK)),
            _const_spec((Q_RANK, B_HEADS * B_QK_DIM)),
            _const_spec((B_HEADS, B_QK_DIM, KV_RANK)),
            _const_spec((Q_RANK, IDX_HEADS * IDX_DIM)),
        ],
        out_specs=[
            pl.BlockSpec((tm, B_HEADS * KV_RANK), lambda i: (i, 0)),
            pl.BlockSpec((IDX_HEADS * IDX_DIM, tm), lambda i: (0, i)),
            pl.BlockSpec((IDX_HEADS, tm), lambda i: (0, i)),
            pl.BlockSpec((tm, 2 * KV_RANK), lambda i: (i, 0)),
            pl.BlockSpec((KV_RANK, tm), lambda i: (0, i)),
            pl.BlockSpec((tm, IDX_DIM), lambda i: (i, 0)),
        ],
        compiler_params=_cparams(("arbitrary",)),
        name="mla_prep",
    )(proj, q_norm, kv_norm, w_uq, w_ukt, w_iq)


def _bias_tab_kernel(bkt_ref, rb_ref, o_ref):
    far = N_BUCKETS // 2 - 1
    for v in range(2):
        bkt = bkt_ref[v]
        for h in range(B_HEADS):
            acc = jnp.zeros(bkt.shape, F32)
            for k in range(N_BUCKETS):
                acc = jnp.where(bkt == k, (rb_ref[k, h] - rb_ref[far, h]) * LOG2E, acc)
            o_ref[v, h] = acc


def _bias_tab(buckets, rel_bias):
    return pl.pallas_call(
        _bias_tab_kernel,
        out_shape=jax.ShapeDtypeStruct((2, B_HEADS, TQ, TKN), F32),
        in_specs=[
            pl.BlockSpec(memory_space=pltpu.VMEM),
            pl.BlockSpec(memory_space=pltpu.SMEM),
        ],
        out_specs=pl.BlockSpec(memory_space=pltpu.VMEM),
        compiler_params=pltpu.CompilerParams(vmem_limit_bytes=VMEM_LIMIT),
        name="bias_tab",
    )(buckets, rel_bias)


def _t5_bucket(rel):
    half = N_BUCKETS // 2
    max_exact = half // 2
    ret = jnp.where(rel > 0, half, 0)
    n = jnp.abs(rel)
    nf = jnp.maximum(n, 1).astype(jnp.float32)
    large = max_exact + (jnp.log(nf / max_exact) / math.log(MAX_DISTANCE / max_exact) * (half - max_exact)).astype(jnp.int32)
    large = jnp.minimum(large, half - 1)
    return ret + jnp.where(n < max_exact, n, large)


def _dsa_kernel(qlat_ref, qidxt_ref, widxt_ref, kidx_ref, ckvt_ref, ckvx_ref, biasd_ref, wuv_ref,
                o_ref, isc_scr, t_scr, lhs_scr, rhs_scr, m_scr, acc_scr, p_scr, al_scr, qp_scr,
                *, top_k):
    qi = pl.program_id(0)
    q0 = qi * TQ
    kf = float(top_k)
    qlane = lax.broadcasted_iota(I32, (1, TQ), 1)
    lim = q0 + (qlane // CHUNK + 1) * CHUNK
    krow = lax.broadcasted_iota(I32, (TKA, TQ), 0)

    def fold(x, op):
        n = x.shape[0] // 8
        x = x.reshape(n, 8, TQ)
        while n > 1:
            n //= 2
            x = op(x[:n], x[n:2 * n])
        return x[0]

    def lanes_x(v, width):
        return jnp.concatenate([v] * (width // LANES), axis=1)

    na = (q0 + TQ + TKA - 1) // TKA
    qt = qidxt_ref[...]
    wt = widxt_ref[...]
    wp = []
    for p in range(IDX_HEADS // 2):
        qa = qt[(2 * p) * IDX_DIM:(2 * p + 1) * IDX_DIM, :]
        qb = qt[(2 * p + 1) * IDX_DIM:(2 * p + 2) * IDX_DIM, :]
        qp_scr[p] = jnp.concatenate([qa, qb], axis=1)
        wp.append(jnp.concatenate([wt[2 * p:2 * p + 1, :], wt[2 * p + 1:2 * p + 2, :]], axis=1))

    def scores(k0):
        kb = kidx_ref[pl.ds(k0, TKA), :]
        acc = jnp.zeros((TKA, TQ), F32)
        for p in range(IDX_HEADS // 2):
            s2 = jnp.maximum(_dot(kb, qp_scr[p]), 0.0) * wp[p]
            acc = acc + (s2[:, 0:TQ] + s2[:, TQ:])
        return acc

    def emit(k0, acc, stats):
        mn, mx, c_ge0, c_gt0 = stats
        mn = jnp.minimum(mn, fold(acc, jnp.minimum))
        mx = jnp.maximum(mx, fold(acc, jnp.maximum))
        a = jnp.where((k0 + krow) < lim, acc, -jnp.inf)
        c_ge0 = c_ge0 + fold(jnp.where(a >= 0.0, 1.0, 0.0), jnp.add)
        c_gt0 = c_gt0 + fold(jnp.where(a > 0.0, 1.0, 0.0), jnp.add)
        isc_scr[pl.ds(k0, TKA), :] = a
        return mn, mx, c_ge0, c_gt0

    def phase_a(jb, stats):
        k0 = pl.multiple_of(jb * (2 * TKA), 2 * TKA)
        a0 = scores(k0)
        a1 = scores(k0 + TKA)
        return emit(k0 + TKA, a1, emit(k0, a0, stats))

    zeros8 = jnp.zeros((8, TQ), F32)
    mn, mx, pc_ge0, pc_gt0 = lax.fori_loop(
        0, (na + 1) // 2, phase_a,
        (jnp.full((8, TQ), jnp.inf, F32), jnp.full((8, TQ), -jnp.inf, F32), zeros8, zeros8))

    def count(pred):
        def body(g, cnt):
            k0 = pl.multiple_of(g * TKA, TKA)
            x = isc_scr[pl.ds(k0, TKA), :]
            return cnt + fold(jnp.where(pred(x, k0 + krow), 1.0, 0.0), jnp.add)
        return jnp.sum(lax.fori_loop(0, na, body, zeros8), axis=0, keepdims=True)

    t_scr[...] = jnp.full(t_scr.shape, -jnp.inf, F32)

    @pl.when(q0 + CHUNK > top_k)
    def _():
        def row_total(part):
            return jnp.sum(part, axis=0, keepdims=True)

        rmin = jnp.min(mn, axis=0, keepdims=True)
        rmax = jnp.max(mx, axis=0, keepdims=True)
        n_adm = lim.astype(F32)
        c_ge0, c_gt0 = row_total(pc_ge0), row_total(pc_gt0)
        above = c_gt0 >= kf
        below = c_ge0 < kf
        tiny = float(jnp.finfo(F32).tiny)
        hi_top = rmax + (rmax - rmin) + jnp.abs(rmax) * 1e-3 + 1e-30
        lo0 = jnp.where(below, rmin, 0.0)
        clo0 = jnp.where(below, n_adm, c_ge0)
        hi0 = jnp.where(below, 0.0, jnp.where(above, hi_top, tiny))
        chi0 = jnp.where(below, c_ge0, jnp.where(above, 0.0, c_gt0))
        log_k = math.log(kf)

        def plan(it, lo, hi, clo, chi, done):
            bis = 0.5 * lo + 0.5 * hi
            act = jnp.where(jnp.logical_and(done == 0.0, jnp.logical_and(bis > lo, bis < hi)), 1.0, 0.0)
            llo = jnp.log(clo)
            lhi = jnp.log(jnp.maximum(chi, 0.5))
            t = jnp.clip((llo - log_k) * pl.reciprocal(llo - lhi, approx=True), 0.03, 0.97)
            guess = lo + t * (hi - lo)
            use = jnp.logical_and(jnp.logical_and(guess > lo, guess < hi), it % 3 != 2)
            return act, jnp.sum(act), jnp.where(use, guess, bis)

        def count_ge(mid):
            def body(g, cnt):
                k0 = pl.multiple_of(g * TKA, TKA)
                return cnt + fold(jnp.where(isc_scr[pl.ds(k0, TKA), :] >= mid, 1.0, 0.0), jnp.add)

            return row_total(lax.fori_loop(0, na, body, zeros8))

        def cond(st):
            it, nact = st[0], st[1]
            return jnp.logical_and(it < MAX_BISECT, nact > 0.0)

        def body(st):
            it, nact, lo, hi, clo, chi, done, act, mid = st
            c = count_ge(mid)
            on = act > 0.0
            up = jnp.logical_and(on, c >= kf)
            dn = jnp.logical_and(on, c < kf)
            lo = jnp.where(up, mid, lo)
            clo = jnp.where(up, c, clo)
            hi = jnp.where(dn, mid, hi)
            chi = jnp.where(dn, c, chi)
            done = jnp.where(jnp.logical_and(on, c == kf), 1.0, done)
            act, nact, mid = plan(it + 1, lo, hi, clo, chi, done)
            return it + 1, nact, lo, hi, clo, chi, done, act, mid

        done0 = jnp.where(c_ge0 == kf, 1.0, 0.0)
        act0, nact0, mid0 = plan(0, lo0, hi0, clo0, chi0, done0)
        st = lax.while_loop(cond, body, (jnp.int32(0), nact0, lo0, hi0, clo0, chi0, done0, act0, mid0))
        t_lo, t_hi, done = st[2], st[3], st[6]
        t_scr[0:1, :] = t_lo

        n_tie = jnp.sum(jnp.where(done == 0.0, 1.0, 0.0))

        @pl.when(n_tie > 0.0)
        def _():
            lo, hi, tie = t_lo, t_hi, done == 0.0
            r = kf - count(lambda x, idx: x >= hi)
            p = jnp.zeros((1, TQ), I32)
            nbits = max(1, (isc_scr.shape[0] - 1).bit_length())
            for bit in range(nbits - 1, -1, -1):
                cand = p + (1 << bit)
                f = count(lambda x, idx: jnp.logical_and(jnp.logical_and(x >= lo, x < hi), idx < cand))
                p = jnp.where(f < r, cand, p)

            def drop(g, carry):
                k0 = pl.multiple_of(g * TKA, TKA)
                x = isc_scr[pl.ds(k0, TKA), :]
                ex = jnp.logical_and(jnp.logical_and(tie, (k0 + krow) > p),
                                     jnp.logical_and(x >= lo, x < hi))
                isc_scr[pl.ds(k0, TKA), :] = jnp.where(ex, -jnp.inf, x)
                return carry

            lax.fori_loop(0, na, drop, 0)

    thr = t_scr[0:1, :]
    for h in range(B_HEADS):
        lhs_scr[h * TQ:(h + 1) * TQ, 0:KV_RANK] = qlat_ref[:, h * KV_RANK:(h + 1) * KV_RANK]

    @pl.when(qi == 0)
    def _():
        eye = jnp.where(lax.broadcasted_iota(I32, (TQ, TQ), 0) == lax.broadcasted_iota(I32, (TQ, TQ), 1),
                        1.0, 0.0).astype(BF16)
        for h in range(B_HEADS):
            lhs_scr[h * TQ:(h + 1) * TQ, KV_RANK:] = eye

    m_scr[...] = jnp.full(m_scr.shape, NEG, F32)
    acc_scr[...] = jnp.zeros_like(acc_scr)

    def accumulate(h, k0_prev, w_prev):
        rs = slice(h * TQ, (h + 1) * TQ)
        kv = ckvx_ref[pl.ds(k0_prev, w_prev), :]
        acc_scr[rs, :] = (acc_scr[rs, :] * lanes_x(al_scr[rs, :], 2 * KV_RANK)
                          + _dot(p_scr[rs, 0:w_prev], kv))

    def stage(k0_prev, w_prev, k0, width, key_lim, with_bias):
        x = isc_scr[pl.ds(k0, width), :]
        sel = jnp.logical_and(x >= thr, (k0 + krow[0:width, :]) < key_lim)
        rhs_scr[0:KV_RANK, 0:width] = ckvt_ref[:, pl.ds(k0, width)]
        rhs_scr[KV_RANK:, 0:width] = jnp.where(sel, 0.0, NEG).T.astype(BF16)
        rhs = rhs_scr[:, 0:width]
        for h in range(B_HEADS):
            rs = slice(h * TQ, (h + 1) * TQ)
            if w_prev:
                accumulate(h, k0_prev, w_prev)
            s = _dot(lhs_scr[rs, :], rhs)
            if with_bias:
                s = s + biasd_ref[0, h]
            m_old = m_scr[rs, :]
            m_new = jnp.maximum(m_old, jnp.max(s, axis=1, keepdims=True))
            al_scr[rs, :] = jnp.exp2(m_old - m_new)
            p_scr[rs, 0:width] = jnp.exp2(s - lanes_x(m_new, width)).astype(BF16)
            m_scr[rs, :] = m_new

    far_lim = jnp.maximum(q0 - TQ, 0)
    nf = (far_lim + TK - 1) // TK

    @pl.when(nf == 0)
    def _():
        p_scr[...] = jnp.zeros_like(p_scr)
        al_scr[...] = jnp.ones_like(al_scr)

    @pl.when(nf > 0)
    def _():
        stage(0, 0, 0, TK, far_lim, False)

    def far_stage(jb):
        stage(pl.multiple_of((jb - 1) * TK, TK), TK, pl.multiple_of(jb * TK, TK), TK, far_lim, False)

    def far_pair(jj, carry):
        far_stage(2 * jj + 1)
        far_stage(2 * jj + 2)
        return carry

    lax.fori_loop(0, (nf - 1) // 2, far_pair, 0)

    @pl.when(jnp.logical_and(nf >= 2, nf % 2 == 0))
    def _():
        far_stage(nf - 1)

    k0_last = pl.multiple_of(jnp.maximum(nf - 1, 0) * TK, TK)
    k0_near = pl.multiple_of(far_lim, TQ)
    stage(k0_last, TK, k0_near, TKN, lim, True)
    for h in range(B_HEADS):
        accumulate(h, k0_near, TKN)

    for h in range(B_HEADS):
        rs = slice(h * TQ, (h + 1) * TQ)
        a = acc_scr[rs, :]
        olat = (a[:, 0:KV_RANK] / a[:, KV_RANK:]).astype(BF16)
        o_ref[:, h * B_V_DIM:(h + 1) * B_V_DIM] = _dot(olat, wuv_ref[h]).astype(o_ref.dtype)


def _dsa(qlat, qidxt, widxt, kidx, ckvt, ckvx, biasd, w_uv, top_k):
    s = qlat.shape[0]
    assert s % (2 * TKA) == 0 and TK == TKA and top_k % TQ == 0
    return pl.pallas_call(
        functools.partial(_dsa_kernel, top_k=top_k),
        out_shape=jax.ShapeDtypeStruct((s, B_WIDTH), BF16),
        grid=(s // TQ,),
        in_specs=[
            pl.BlockSpec((TQ, B_HEADS * KV_RANK), lambda i: (i, 0)),
            pl.BlockSpec((IDX_HEADS * IDX_DIM, TQ), lambda i: (0, i)),
            pl.BlockSpec((IDX_HEADS, TQ), lambda i: (0, i)),
            _const_spec((s, IDX_DIM)),
            _const_spec((KV_RANK, s)),
            _const_spec((s, 2 * KV_RANK)),
            pl.BlockSpec((1, B_HEADS, TQ, TKN), lambda i: (jnp.minimum(i, 1), 0, 0, 0)),
            _const_spec((B_HEADS, KV_RANK, B_V_DIM)),
        ],
        out_specs=pl.BlockSpec((TQ, B_WIDTH), lambda i: (i, 0)),
        scratch_shapes=[
            pltpu.VMEM((s, TQ), F32),
            pltpu.VMEM((8, TQ), F32),
            pltpu.VMEM((B_HEADS * TQ, 2 * KV_RANK), BF16),
            pltpu.VMEM((2 * KV_RANK, TK), BF16),
            pltpu.VMEM((B_HEADS * TQ, LANES), F32),
            pltpu.VMEM((B_HEADS * TQ, 2 * KV_RANK), F32),
            pltpu.VMEM((B_HEADS * TQ, TK), BF16),
            pltpu.VMEM((B_HEADS * TQ, LANES), F32),
            pltpu.VMEM((IDX_HEADS // 2, IDX_DIM, 2 * TQ), BF16),
        ],
        compiler_params=_cparams(("arbitrary",)),
        name="dsa",
    )(qlat, qidxt, widxt, kidx, ckvt, ckvx, biasd, w_uv)


def _merge_kernel(x_ref, hg_ref, ob_ref, ga_ref, gb_ref, wa_ref, wb_ref, wo_ref, fn_ref,
                  x1_ref, h2_ref):
    ya = _dot(hg_ref[...], wa_ref[...])
    yb = _dot(ob_ref[...], wb_ref[...])
    mix = _sigmoid(ga_ref[...]) * ya + _sigmoid(gb_ref[...]) * yb
    x1 = x_ref[...] + _dot(mix.astype(BF16), wo_ref[...])
    x1_ref[...] = x1
    h2 = x1 * lax.rsqrt(jnp.mean(x1 * x1, axis=-1, keepdims=True) + EPS) * fn_ref[...]
    h2_ref[...] = h2.astype(BF16)


def _merge(x2, hg, ob, proj, w_a, w_b, w_o, ffn_norm, tm):
    s, d = x2.shape
    gblk = COL_GATE // d
    row = lambda i: (i, 0)
    return pl.pallas_call(
        _merge_kernel,
        out_shape=[jax.ShapeDtypeStruct((s, d), F32), jax.ShapeDtypeStruct((s, d), BF16)],
        grid=(s // tm,),
        in_specs=[
            pl.BlockSpec((tm, d), row),
            pl.BlockSpec((tm, A_WIDTH), row),
            pl.BlockSpec((tm, B_WIDTH), row),
            pl.BlockSpec((tm, d), lambda i: (i, gblk)),
            pl.BlockSpec((tm, d), lambda i: (i, gblk + 1)),
            _const_spec((A_WIDTH, d)),
            _const_spec((B_WIDTH, d)),
            _const_spec((d, d)),
            _const_spec((1, d)),
        ],
        out_specs=[pl.BlockSpec((tm, d), row), pl.BlockSpec((tm, d), row)],
        compiler_params=_cparams(("arbitrary",)),
        name="merge",
    )(x2, hg, ob, proj, proj, w_a, w_b, w_o, ffn_norm)


def _ffn_kernel(x1_ref, h2_ref, wup_ref, cw_ref, cb_ref, wdn_ref, fin_ref, o_ref,
                tail_scr, u_scr, act_scr, *, cbw):
    tm = h2_ref.shape[0]
    dff = wdn_ref.shape[0]
    halo = 8

    @pl.when(pl.program_id(0) == 0)
    def _():
        tail_scr[...] = jnp.zeros_like(tail_scr)

    h2 = h2_ref[...]

    def conv_cols(c0):
        u = _dot(h2, wup_ref[:, c0:c0 + cbw])
        u_scr[0:halo, :] = tail_scr[:, c0:c0 + cbw]
        u_scr[halo:, :] = u
        tail_scr[:, c0:c0 + cbw] = u[tm - halo:, :]
        cw = cw_ref[:, c0:c0 + cbw]
        return (cw[2:3] * u + cw[1:2] * u_scr[halo - 1:halo - 1 + tm, :]
                + cw[0:1] * u_scr[halo - 2:halo - 2 + tm, :] + cb_ref[:, c0:c0 + cbw])

    for j in range(dff // cbw):
        a = conv_cols(j * cbw)
        v = conv_cols(dff + j * cbw)
        act_scr[:, j * cbw:(j + 1) * cbw] = (a * _sigmoid(a) * v).astype(BF16)

    y = x1_ref[...] + _dot(act_scr[...], wdn_ref[...])
    o_ref[...] = y * lax.rsqrt(jnp.mean(y * y, axis=-1, keepdims=True) + EPS) * fin_ref[...]


def _ffn(x1, h2, w_up, conv_w, conv_b, w_down, final_norm, tm, cbw):
    s, d = x1.shape
    dff = w_down.shape[0]
    row = lambda i: (i, 0)
    return pl.pallas_call(
        functools.partial(_ffn_kernel, cbw=cbw),
        out_shape=jax.ShapeDtypeStruct((s, d), F32),
        grid=(s // tm,),
        in_specs=[
            pl.BlockSpec((tm, d), row),
            pl.BlockSpec((tm, d), row),
            _const_spec((d, 2 * dff)),
            _const_spec((CONV_WIDTH, 2 * dff)),
            _const_spec((1, 2 * dff)),
            _const_spec((dff, d)),
            _const_spec((1, d)),
        ],
        out_specs=pl.BlockSpec((tm, d), row),
        scratch_shapes=[
            pltpu.VMEM((8, 2 * dff), F32),
            pltpu.VMEM((tm + 8, cbw), F32),
            pltpu.VMEM((tm, dff), BF16),
        ],
        compiler_params=_cparams(("arbitrary",)),
        name="ffn",
    )(x1, h2, w_up, conv_w, conv_b, w_down, final_norm)


def _row_tile(s, want):
    t = min(want, s)
    while s % t:
        t //= 2
    return t


def kernel(x, attn_norm, w_in, lower_bounds, hgrn_out_norm, q_norm, kv_norm, w_uq, w_uk, w_uv, w_iq,
           rel_bias, w_branch_a, w_branch_b, w_o, ffn_norm, w_up, conv_w, conv_b, w_down, final_norm):
    bsz, s, d = x.shape
    assert bsz == 1 and attn_norm.shape[0] == 1 and lower_bounds.shape[0] == 2
    x2 = x.reshape(s, d)
    top_k = min(TOPK_MAX, s // 4)

    wi = w_in[0]
    n_small = Q_RANK + KV_RANK + IDX_DIM + IDX_HEADS
    w_pad = jnp.concatenate(
        [wi[:, :COL_GATE], wi[:, COL_GATE + n_small:], wi[:, COL_GATE:COL_GATE + n_small],
         jnp.zeros((d, SMALL_W - n_small), wi.dtype)], axis=1).astype(BF16)

    proj = _inproj(x2, attn_norm, w_pad, _row_tile(s, 1024), N_PROJ // 4)

    hg = _hgrn(proj, lower_bounds, hgrn_out_norm, _row_tile(s, 512), 4)

    qlat, qidxt, widxt, ckvx, ckvt, kidx = _mla_prep(
        proj, q_norm, kv_norm, w_uq[0].astype(BF16), jnp.swapaxes(w_uk[0], 1, 2).astype(BF16),
        w_iq[0].astype(BF16), _row_tile(s, 512))

    t_loc = jnp.arange(TQ, dtype=jnp.int32)[None, :, None]
    s_loc = jnp.arange(TKN, dtype=jnp.int32)[None, None, :]
    off = jnp.array([0, -TQ], jnp.int32)[:, None, None]
    buckets = _t5_bucket(s_loc - t_loc + off)
    biasd = _bias_tab(buckets, rel_bias)

    ob = _dsa(qlat, qidxt, widxt, kidx, ckvt, ckvx, biasd, w_uv[0].astype(BF16), top_k)

    x1, h2 = _merge(x2, hg, ob, proj, w_branch_a[0].astype(BF16), w_branch_b[0].astype(BF16),
                    w_o[0].astype(BF16), ffn_norm, _row_tile(s, 512))

    out = _ffn(x1, h2, w_up[0].astype(BF16), conv_w[0], conv_b, w_down[0].astype(BF16),
               final_norm.reshape(1, d), _row_tile(s, 512), 256)
    return out.reshape(bsz, s, d)
```

```python
import functools
import math

import jax
import jax.numpy as jnp
from jax import lax
from jax.experimental import pallas as pl
from jax.experimental.pallas import tpu as pltpu

F32 = jnp.float32
BF16 = jnp.bfloat16
I32 = jnp.int32

EPS = 1e-6
CHUNK = 64
A_HEADS = 8
A_HEAD_DIM = 128
A_WIDTH = A_HEADS * A_HEAD_DIM
B_HEADS = 16
B_QK_DIM = 64
B_V_DIM = 64
B_WIDTH = B_HEADS * B_V_DIM
Q_RANK = 256
KV_RANK = 128
IDX_HEADS = 8
IDX_DIM = 64
TOPK_MAX = 256
N_BUCKETS = 32
MAX_DISTANCE = 128
CONV_WIDTH = 3

LANES = 128
VMEM_LIMIT = 56 * 1024 * 1024

COL_QFIG = 0
COL_GATE = 4 * A_WIDTH
COL_SMALL = COL_GATE + 2048
SMALL_W = 512
N_PROJ = COL_SMALL + SMALL_W

TQ = 128
TKA = 512
TK = 512
TKN = 2 * TQ
NEG = -1e30
EXP_SAFE = 80.0
MAX_BISECT = 1024
LOG2E = 1.4426950408889634


def _cparams(sem):
    return pltpu.CompilerParams(dimension_semantics=sem, vmem_limit_bytes=VMEM_LIMIT)


def _const_spec(shape):
    nd = len(shape)
    return pl.BlockSpec(shape, lambda *_: (0,) * nd, pipeline_mode=pl.Buffered(1))


def _dot(a, b):
    return jnp.dot(a, b, preferred_element_type=F32)


def _dot_nt(a, b):
    return lax.dot_general(a, b, (((1,), (1,)), ((), ())), preferred_element_type=F32)


def _dot_tn(a, b):
    return lax.dot_general(a, b, (((0,), (0,)), ((), ())), preferred_element_type=F32)


def _sigmoid(x):
    return jax.nn.sigmoid(x)


def _inproj_kernel(x_ref, g_ref, w_ref, o_ref, h_scr):
    @pl.when(pl.program_id(1) == 0)
    def _():
        x = x_ref[...]
        y = x * lax.rsqrt(jnp.mean(x * x, axis=-1, keepdims=True) + EPS) * g_ref[...]
        h_scr[...] = y.astype(BF16)

    o_ref[...] = _dot(h_scr[...], w_ref[...])


def _inproj(x2, gain, w_pad, tm, tn):
    s, d = x2.shape
    n = w_pad.shape[1]
    return pl.pallas_call(
        _inproj_kernel,
        out_shape=jax.ShapeDtypeStruct((s, n), F32),
        grid=(s // tm, n // tn),
        in_specs=[
            pl.BlockSpec((tm, d), lambda i, j: (i, 0)),
            pl.BlockSpec((1, d), lambda i, j: (0, 0)),
            pl.BlockSpec((d, tn), lambda i, j: (0, j)),
        ],
        out_specs=pl.BlockSpec((tm, tn), lambda i, j: (i, j)),
        scratch_shapes=[pltpu.VMEM((tm, d), BF16)],
        compiler_params=_cparams(("arbitrary", "arbitrary")),
        name="inproj",
    )(x2, gain, w_pad)


def _hgrn_kernel(q_ref, f_ref, i_ref, g_ref, lbp_ref, on_ref, o_ref,
                 st_ref, b_scr, k_scr, qs_scr, att_scr, *, nch):
    c_ = CHUNK
    half = c_ // 2

    @pl.when(pl.program_id(1) == 0)
    def _():
        st_ref[...] = jnp.zeros_like(st_ref)

    l0 = lbp_ref[0:1, :]
    l1 = lbp_ref[1:2, :]
    lm = jnp.maximum(l0, l1)
    e0 = jnp.exp(l0 - lm)
    e1 = jnp.exp(l1 - lm)
    lb = e0 / (e0 + e1)

    row = lax.broadcasted_iota(I32, (c_, c_), 0)
    col = lax.broadcasted_iota(I32, (c_, c_), 1)
    tri = row >= col
    tri_bf = jnp.where(tri, 1.0, 0.0).astype(BF16)

    q = q_ref[...]
    qs_scr[...] = q * _sigmoid(q)
    fg = lb + (1.0 - lb) * _sigmoid(f_ref[...])
    k_scr[...] = 1.0 - fg
    lg = jnp.log(fg)

    d = A_HEAD_DIM
    hpg = q_ref.shape[1] // d
    heads = [slice(hh * d, (hh + 1) * d) for hh in range(hpg)]

    dev = jnp.zeros((1, 1), F32)
    for c in range(nch):
        l = lg[c * c_:(c + 1) * c_]
        hi = l.astype(BF16)
        r1 = l - hi.astype(F32)
        mid = r1.astype(BF16)
        lo = (r1 - mid.astype(F32)).astype(BF16)
        b = _dot(tri_bf, hi) + _dot(tri_bf, mid) + _dot(tri_bf, lo)
        b_scr[c * c_:(c + 1) * c_, :] = b
        rho = b[half - 1:half, :]
        dev = jnp.maximum(dev, jnp.max(jnp.abs(b - rho), keepdims=True))
    safe = dev[0, 0] <= EXP_SAFE

    @pl.when(safe)
    def _():
        for c in range(nch):
            sl = slice(c * c_, (c + 1) * c_)
            b = b_scr[sl, :]
            rho = b[half - 1:half, :]
            qt = (qs_scr[sl, :] * jnp.exp(b - rho)).astype(BF16)
            kt = (k_scr[sl, :] * jnp.exp(rho - b)).astype(BF16)
            for hh, hs in enumerate(heads):
                att_scr[hh, c] = jnp.where(tri, _dot_nt(qt[:, hs], kt[:, hs]), 0.0)

    @pl.when(jnp.logical_not(safe))
    def _():
        trow = lax.broadcasted_iota(I32, (c_, d), 0)
        for c in range(nch):
            sl = slice(c * c_, (c + 1) * c_)
            for hh, hs in enumerate(heads):
                b = b_scr[sl, hs]
                qv = qs_scr[sl, hs]

                def body(s, att):
                    bs = b_scr[pl.ds(c * c_ + s, 1), :][:, hs]
                    ks = k_scr[pl.ds(c * c_ + s, 1), :][:, hs]
                    dd = jnp.where(trow >= s, b - bs, -jnp.inf)
                    colv = jnp.sum(qv * ks * jnp.exp(dd), axis=-1, keepdims=True)
                    return jnp.where(col == s, colv, att)

                att_scr[hh, c] = lax.fori_loop(0, c_, body, jnp.zeros((c_, c_), F32))

    on = on_ref[...]
    for c in range(nch):
        sl = slice(c * c_, (c + 1) * c_)
        b = b_scr[sl, :]
        bend = b[c_ - 1:c_, :]
        v = i_ref[sl, :].astype(BF16)
        qe = (qs_scr[sl, :] * jnp.exp(b)).astype(BF16)
        kd = (k_scr[sl, :] * jnp.exp(bend - b)).astype(BF16)
        decay = jnp.exp(bend)
        g = g_ref[sl, :]
        gate = g * _sigmoid(g)
        for hh, hs in enumerate(heads):
            st = st_ref[hh]
            o = _dot_nt(qe[:, hs], st.astype(BF16)) + _dot(att_scr[hh, c].astype(BF16), v[:, hs])
            st_ref[hh] = st * decay[:, hs] + _dot_tn(v[:, hs], kd[:, hs])
            o = o * lax.rsqrt(jnp.mean(o * o, axis=-1, keepdims=True) + EPS) * on
            o_ref[sl, hs] = (o * gate[:, hs]).astype(o_ref.dtype)


def _hgrn(proj, lower_bounds, out_norm, t_rows, hpg):
    s = proj.shape[0]
    nch = t_rows // CHUNK
    d = A_HEAD_DIM
    w = hpg * d
    nb = COL_QFIG // w

    def colspec(off):
        return pl.BlockSpec((t_rows, w), lambda h, i, off=off: (i, nb + off * (A_HEADS // hpg) + h))

    return pl.pallas_call(
        functools.partial(_hgrn_kernel, nch=nch),
        out_shape=jax.ShapeDtypeStruct((s, A_WIDTH), BF16),
        grid=(A_HEADS // hpg, s // t_rows),
        in_specs=[
            colspec(0), colspec(1), colspec(2), colspec(3),
            pl.BlockSpec((2, w), lambda h, i: (0, h)),
            pl.BlockSpec((1, d), lambda h, i: (0, 0)),
        ],
        out_specs=pl.BlockSpec((t_rows, w), lambda h, i: (i, h)),
        scratch_shapes=[
            pltpu.VMEM((hpg, d, d), F32),
            pltpu.VMEM((t_rows, w), F32),
            pltpu.VMEM((t_rows, w), F32),
            pltpu.VMEM((t_rows, w), F32),
            pltpu.VMEM((hpg, nch, CHUNK, CHUNK), F32),
        ],
        compiler_params=_cparams(("arbitrary", "arbitrary")),
        name="hgrn",
    )(proj, proj, proj, proj, lower_bounds, out_norm)


def _mla_prep_kernel(p_ref, qn_ref, kn_ref, wuq_ref, wukt_ref, wiq_ref,
                     qlat_ref, qidxt_ref, widxt_ref, ckvx_ref, ckvt_ref, kidx_ref):
    p = p_ref[...]
    cq = p[:, 0:Q_RANK]
    cq = cq * lax.rsqrt(jnp.mean(cq * cq, axis=-1, keepdims=True) + EPS) * qn_ref[...]
    cqb = cq.astype(BF16)
    ckv = p[:, Q_RANK:Q_RANK + KV_RANK]
    ckv = ckv * lax.rsqrt(jnp.mean(ckv * ckv, axis=-1, keepdims=True) + EPS) * kn_ref[...]

    q = _dot(cqb, wuq_ref[...])
    scale = B_QK_DIM ** -0.5 * LOG2E
    for h in range(B_HEADS):
        qh = q[:, h * B_QK_DIM:(h + 1) * B_QK_DIM].astype(BF16)
        qlat_ref[:, h * KV_RANK:(h + 1) * KV_RANK] = (_dot(qh, wukt_ref[h]) * scale).astype(BF16)
    qidxt_ref[...] = _dot(cqb, wiq_ref[...]).T.astype(BF16)

    rest = p[:, Q_RANK + KV_RANK:]
    widxt_ref[...] = rest.T[IDX_DIM:IDX_DIM + IDX_HEADS, :] * ((IDX_HEADS * IDX_DIM) ** -0.5)
    ckvx_ref[:, 0:KV_RANK] = ckv.astype(BF16)
    ckvx_ref[:, KV_RANK:] = jnp.ones((p.shape[0], KV_RANK), BF16)
    ckvt_ref[...] = ckv.T.astype(BF16)
    kidx_ref[...] = rest[:, 0:IDX_DIM].astype(BF16)


def _mla_prep(proj, q_norm, kv_norm, w_uq, w_ukt, w_iq, tm):
    s = proj.shape[0]
    return pl.pallas_call(
        _mla_prep_kernel,
        out_shape=[
            jax.ShapeDtypeStruct((s, B_HEADS * KV_RANK), BF16),
            jax.ShapeDtypeStruct((IDX_HEADS * IDX_DIM, s), BF16),
            jax.ShapeDtypeStruct((IDX_HEADS, s), F32),
            jax.ShapeDtypeStruct((s, 2 * KV_RANK), BF16),
            jax.ShapeDtypeStruct((KV_RANK, s), BF16),
            jax.ShapeDtypeStruct((s, IDX_DIM), BF16),
        ],
        grid=(s // tm,),
        in_specs=[
            pl.BlockSpec((tm, SMALL_W), lambda i: (i, COL_SMALL // SMALL_W)),
            _const_spec((1, Q_RANK)),
            _const_spec((1, KV_RANK)),
            _const_spec((Q_RANK, B_HEADS * B_QK_DIM)),
            _const_spec((B_HEADS, B_QK_DIM, KV_RANK)),
            _const_spec((Q_RANK, IDX_HEADS * IDX_DIM)),
        ],
        out_specs=[
            pl.BlockSpec((tm, B_HEADS * KV_RANK), lambda i: (i, 0)),
            pl.BlockSpec((IDX_HEADS * IDX_DIM, tm), lambda i: (0, i)),
            pl.BlockSpec((IDX_HEADS, tm), lambda i: (0, i)),
            pl.BlockSpec((tm, 2 * KV_RANK), lambda i: (i, 0)),
            pl.BlockSpec((KV_RANK, tm), lambda i: (0, i)),
            pl.BlockSpec((tm, IDX_DIM), lambda i: (i, 0)),
        ],
        compiler_params=_cparams(("arbitrary",)),
        name="mla_prep",
    )(proj, q_norm, kv_norm, w_uq, w_ukt, w_iq)


def _bias_tab_kernel(bkt_ref, rb_ref, o_ref):
    far = N_BUCKETS // 2 - 1
    for v in range(2):
        bkt = bkt_ref[v]
        for h in range(B_HEADS):
            acc = jnp.zeros(bkt.shape, F32)
            for k in range(N_BUCKETS):
                acc = jnp.where(bkt == k, (rb_ref[k, h] - rb_ref[far, h]) * LOG2E, acc)
            o_ref[v, h] = acc


def _bias_tab(buckets, rel_bias):
    return pl.pallas_call(
        _bias_tab_kernel,
        out_shape=jax.ShapeDtypeStruct((2, B_HEADS, TQ, TKN), F32),
        in_specs=[
            pl.BlockSpec(memory_space=pltpu.VMEM),
            pl.BlockSpec(memory_space=pltpu.SMEM),
        ],
        out_specs=pl.BlockSpec(memory_space=pltpu.VMEM),
        compiler_params=pltpu.CompilerParams(vmem_limit_bytes=VMEM_LIMIT),
        name="bias_tab",
    )(buckets, rel_bias)


def _t5_bucket(rel):
    half = N_BUCKETS // 2
    max_exact = half // 2
    ret = jnp.where(rel > 0, half, 0)
    n = jnp.abs(rel)
    nf = jnp.maximum(n, 1).astype(jnp.float32)
    large = max_exact + (jnp.log(nf / max_exact) / math.log(MAX_DISTANCE / max_exact) * (half - max_exact)).astype(jnp.int32)
    large = jnp.minimum(large, half - 1)
    return ret + jnp.where(n < max_exact, n, large)


def _dsa_kernel(qlat_ref, qidxt_ref, widxt_ref, kidx_ref, ckvt_ref, ckvx_ref, biasd_ref, wuv_ref,
                o_ref, isc_scr, t_scr, lhs_scr, rhs_scr, m_scr, acc_scr, p_scr, al_scr, qp_scr,
                *, top_k):
    qi = pl.program_id(0)
    q0 = qi * TQ
    kf = float(top_k)
    qlane = lax.broadcasted_iota(I32, (1, TQ), 1)
    lim = q0 + (qlane // CHUNK + 1) * CHUNK
    krow = lax.broadcasted_iota(I32, (TKA, TQ), 0)

    def fold(x, op):
        n = x.shape[0] // 8
        x = x.reshape(n, 8, TQ)
        while n > 1:
            n //= 2
            x = op(x[:n], x[n:2 * n])
        return x[0]

    def lanes_x(v, width):
        return jnp.concatenate([v] * (width // LANES), axis=1)

    na = (q0 + TQ + TKA - 1) // TKA
    qt = qidxt_ref[...]
    wt = widxt_ref[...]
    wp = []
    for p in range(IDX_HEADS // 2):
        qa = qt[(2 * p) * IDX_DIM:(2 * p + 1) * IDX_DIM, :]
        qb = qt[(2 * p + 1) * IDX_DIM:(2 * p + 2) * IDX_DIM, :]
        qp_scr[p] = jnp.concatenate([qa, qb], axis=1)
        wp.append(jnp.concatenate([wt[2 * p:2 * p + 1, :], wt[2 * p + 1:2 * p + 2, :]], axis=1))

    def scores(k0):
        kb = kidx_ref[pl.ds(k0, TKA), :]
        acc = jnp.zeros((TKA, TQ), F32)
        for p in range(IDX_HEADS // 2):
            s2 = jnp.maximum(_dot(kb, qp_scr[p]), 0.0) * wp[p]
            acc = acc + (s2[:, 0:TQ] + s2[:, TQ:])
        return acc

    def emit(k0, acc, stats, masked):
        mn, mx, c_ge0, c_gt0 = stats
        mn = jnp.minimum(mn, fold(acc, jnp.minimum))
        mx = jnp.maximum(mx, fold(acc, jnp.maximum))
        a = jnp.where((k0 + krow) < lim, acc, -jnp.inf) if masked else acc
        c_ge0 = c_ge0 + fold(jnp.where(a >= 0.0, 1.0, 0.0), jnp.add)
        c_gt0 = c_gt0 + fold(jnp.where(a > 0.0, 1.0, 0.0), jnp.add)
        isc_scr[pl.ds(k0, TKA), :] = a
        return mn, mx, c_ge0, c_gt0

    def phase_a(jb, stats, masked):
        k0 = pl.multiple_of(jb * (2 * TKA), 2 * TKA)
        a0 = scores(k0)
        a1 = scores(k0 + TKA)
        return emit(k0 + TKA, a1, emit(k0, a0, stats, masked), masked)

    zeros8 = jnp.zeros((8, TQ), F32)
    n_trip = (na + 1) // 2
    stats = lax.fori_loop(
        0, n_trip - 1, functools.partial(phase_a, masked=False),
        (jnp.full((8, TQ), jnp.inf, F32), jnp.full((8, TQ), -jnp.inf, F32), zeros8, zeros8))
    mn, mx, pc_ge0, pc_gt0 = phase_a(n_trip - 1, stats, True)

    def count(pred):
        def body(g, cnt):
            k0 = pl.multiple_of(g * TKA, TKA)
            x = isc_scr[pl.ds(k0, TKA), :]
            return cnt + fold(jnp.where(pred(x, k0 + krow), 1.0, 0.0), jnp.add)
        return jnp.sum(lax.fori_loop(0, na, body, zeros8), axis=0, keepdims=True)

    t_scr[...] = jnp.full(t_scr.shape, -jnp.inf, F32)

    @pl.when(q0 + CHUNK > top_k)
    def _():
        def row_total(part):
            return jnp.sum(part, axis=0, keepdims=True)

        rmin = jnp.min(mn, axis=0, keepdims=True)
        rmax = jnp.max(mx, axis=0, keepdims=True)
        n_adm = lim.astype(F32)
        c_ge0, c_gt0 = row_total(pc_ge0), row_total(pc_gt0)
        above = c_gt0 >= kf
        below = c_ge0 < kf
        tiny = float(jnp.finfo(F32).tiny)
        hi_top = rmax + (rmax - rmin) + jnp.abs(rmax) * 1e-3 + 1e-30
        lo0 = jnp.where(below, rmin, 0.0)
        clo0 = jnp.where(below, n_adm, c_ge0)
        hi0 = jnp.where(below, 0.0, jnp.where(above, hi_top, tiny))
        chi0 = jnp.where(below, c_ge0, jnp.where(above, 0.0, c_gt0))
        log_k = math.log(kf)

        def plan(it, lo, hi, clo, chi, done):
            bis = 0.5 * lo + 0.5 * hi
            act = jnp.where(jnp.logical_and(done == 0.0, jnp.logical_and(bis > lo, bis < hi)), 1.0, 0.0)
            llo = jnp.log(clo)
            lhi = jnp.log(jnp.maximum(chi, 0.5))
            t = jnp.clip((llo - log_k) * pl.reciprocal(llo - lhi, approx=True), 0.03, 0.97)
            guess = lo + t * (hi - lo)
            use = jnp.logical_and(jnp.logical_and(guess > lo, guess < hi), it % 3 != 2)
            return act, jnp.sum(act), jnp.where(use, guess, bis)

        def count_ge(mid):
            def body(g, cnt):
                k0 = pl.multiple_of(g * TKA, TKA)
                return cnt + fold(jnp.where(isc_scr[pl.ds(k0, TKA), :] >= mid, 1.0, 0.0), jnp.add)

            return row_total(lax.fori_loop(0, na, body, zeros8))

        def cond(st):
            it, nact = st[0], st[1]
            return jnp.logical_and(it < MAX_BISECT, nact > 0.0)

        def body(st):
            it, nact, lo, hi, clo, chi, done, act, mid = st
            c = count_ge(mid)
            on = act > 0.0
            up = jnp.logical_and(on, c >= kf)
            dn = jnp.logical_and(on, c < kf)
            lo = jnp.where(up, mid, lo)
            clo = jnp.where(up, c, clo)
            hi = jnp.where(dn, mid, hi)
            chi = jnp.where(dn, c, chi)
            done = jnp.where(jnp.logical_and(on, c == kf), 1.0, done)
            act, nact, mid = plan(it + 1, lo, hi, clo, chi, done)
            return it + 1, nact, lo, hi, clo, chi, done, act, mid

        done0 = jnp.where(c_ge0 == kf, 1.0, 0.0)
        act0, nact0, mid0 = plan(0, lo0, hi0, clo0, chi0, done0)
        st = lax.while_loop(cond, body, (jnp.int32(0), nact0, lo0, hi0, clo0, chi0, done0, act0, mid0))
        t_lo, t_hi, done = st[2], st[3], st[6]
        t_scr[0:1, :] = t_lo

        n_tie = jnp.sum(jnp.where(done == 0.0, 1.0, 0.0))

        @pl.when(n_tie > 0.0)
        def _():
            lo, hi, tie = t_lo, t_hi, done == 0.0
            r = kf - count(lambda x, idx: x >= hi)
            p = jnp.zeros((1, TQ), I32)
            nbits = max(1, (isc_scr.shape[0] - 1).bit_length())
            for bit in range(nbits - 1, -1, -1):
                cand = p + (1 << bit)
                f = count(lambda x, idx: jnp.logical_and(jnp.logical_and(x >= lo, x < hi), idx < cand))
                p = jnp.where(f < r, cand, p)

            def drop(g, carry):
                k0 = pl.multiple_of(g * TKA, TKA)
                x = isc_scr[pl.ds(k0, TKA), :]
                ex = jnp.logical_and(jnp.logical_and(tie, (k0 + krow) > p),
                                     jnp.logical_and(x >= lo, x < hi))
                isc_scr[pl.ds(k0, TKA), :] = jnp.where(ex, -jnp.inf, x)
                return carry

            lax.fori_loop(0, na, drop, 0)

    thr = t_scr[0:1, :]
    for h in range(B_HEADS):
        lhs_scr[h * TQ:(h + 1) * TQ, 0:KV_RANK] = qlat_ref[:, h * KV_RANK:(h + 1) * KV_RANK]

    @pl.when(qi == 0)
    def _():
        eye = jnp.where(lax.broadcasted_iota(I32, (TQ, TQ), 0) == lax.broadcasted_iota(I32, (TQ, TQ), 1),
                        1.0, 0.0).astype(BF16)
        for h in range(B_HEADS):
            lhs_scr[h * TQ:(h + 1) * TQ, KV_RANK:] = eye

    m_scr[...] = jnp.full(m_scr.shape, NEG, F32)
    acc_scr[...] = jnp.zeros_like(acc_scr)

    def accumulate(h, k0_prev, w_prev):
        rs = slice(h * TQ, (h + 1) * TQ)
        kv = ckvx_ref[pl.ds(k0_prev, w_prev), :]
        acc_scr[rs, :] = (acc_scr[rs, :] * lanes_x(al_scr[rs, :], 2 * KV_RANK)
                          + _dot(p_scr[rs, 0:w_prev], kv))

    def stage(k0_prev, w_prev, k0, width, key_lim, with_bias):
        x = isc_scr[pl.ds(k0, width), :]
        sel = jnp.logical_and(x >= thr, (k0 + krow[0:width, :]) < key_lim)
        rhs_scr[0:KV_RANK, 0:width] = ckvt_ref[:, pl.ds(k0, width)]
        rhs_scr[KV_RANK:, 0:width] = jnp.where(sel, 0.0, NEG).T.astype(BF16)
        rhs = rhs_scr[:, 0:width]
        for h in range(B_HEADS):
            rs = slice(h * TQ, (h + 1) * TQ)
            if w_prev:
                accumulate(h, k0_prev, w_prev)
            s = _dot(lhs_scr[rs, :], rhs)
            if with_bias:
                s = s + biasd_ref[0, h]
            m_old = m_scr[rs, :]
            m_new = jnp.maximum(m_old, jnp.max(s, axis=1, keepdims=True))
            al_scr[rs, :] = jnp.exp2(m_old - m_new)
            p_scr[rs, 0:width] = jnp.exp2(s - lanes_x(m_new, width)).astype(BF16)
            m_scr[rs, :] = m_new

    far_lim = jnp.maximum(q0 - TQ, 0)
    nf = (far_lim + TK - 1) // TK

    @pl.when(nf == 0)
    def _():
        p_scr[...] = jnp.zeros_like(p_scr)
        al_scr[...] = jnp.ones_like(al_scr)

    @pl.when(nf > 0)
    def _():
        stage(0, 0, 0, TK, far_lim, False)

    def far_stage(jb):
        stage(pl.multiple_of((jb - 1) * TK, TK), TK, pl.multiple_of(jb * TK, TK), TK, far_lim, False)

    def far_pair(jj, carry):
        far_stage(2 * jj + 1)
        far_stage(2 * jj + 2)
        return carry

    lax.fori_loop(0, (nf - 1) // 2, far_pair, 0)

    @pl.when(jnp.logical_and(nf >= 2, nf % 2 == 0))
    def _():
        far_stage(nf - 1)

    k0_last = pl.multiple_of(jnp.maximum(nf - 1, 0) * TK, TK)
    k0_near = pl.multiple_of(far_lim, TQ)
    stage(k0_last, TK, k0_near, TKN, lim, True)
    for h in range(B_HEADS):
        accumulate(h, k0_near, TKN)

    for h in range(B_HEADS):
        rs = slice(h * TQ, (h + 1) * TQ)
        a = acc_scr[rs, :]
        olat = (a[:, 0:KV_RANK] / a[:, KV_RANK:]).astype(BF16)
        o_ref[:, h * B_V_DIM:(h + 1) * B_V_DIM] = _dot(olat, wuv_ref[h]).astype(o_ref.dtype)


def _dsa(qlat, qidxt, widxt, kidx, ckvt, ckvx, biasd, w_uv, top_k):
    s = qlat.shape[0]
    assert s % (2 * TKA) == 0 and TK == TKA and top_k % TQ == 0
    return pl.pallas_call(
        functools.partial(_dsa_kernel, top_k=top_k),
        out_shape=jax.ShapeDtypeStruct((s, B_WIDTH), BF16),
        grid=(s // TQ,),
        in_specs=[
            pl.BlockSpec((TQ, B_HEADS * KV_RANK), lambda i: (i, 0)),
            pl.BlockSpec((IDX_HEADS * IDX_DIM, TQ), lambda i: (0, i)),
            pl.BlockSpec((IDX_HEADS, TQ), lambda i: (0, i)),
            _const_spec((s, IDX_DIM)),
            _const_spec((KV_RANK, s)),
            _const_spec((s, 2 * KV_RANK)),
            pl.BlockSpec((1, B_HEADS, TQ, TKN), lambda i: (jnp.minimum(i, 1), 0, 0, 0)),
            _const_spec((B_HEADS, KV_RANK, B_V_DIM)),
        ],
        out_specs=pl.BlockSpec((TQ, B_WIDTH), lambda i: (i, 0)),
        scratch_shapes=[
            pltpu.VMEM((s, TQ), F32),
            pltpu.VMEM((8, TQ), F32),
            pltpu.VMEM((B_HEADS * TQ, 2 * KV_RANK), BF16),
            pltpu.VMEM((2 * KV_RANK, TK), BF16),
            pltpu.VMEM((B_HEADS * TQ, LANES), F32),
            pltpu.VMEM((B_HEADS * TQ, 2 * KV_RANK), F32),
            pltpu.VMEM((B_HEADS * TQ, TK), BF16),
            pltpu.VMEM((B_HEADS * TQ, LANES), F32),
            pltpu.VMEM((IDX_HEADS // 2, IDX_DIM, 2 * TQ), BF16),
        ],
        compiler_params=_cparams(("arbitrary",)),
        name="dsa",
    )(qlat, qidxt, widxt, kidx, ckvt, ckvx, biasd, w_uv)


def _merge_kernel(x_ref, hg_ref, ob_ref, ga_ref, gb_ref, wa_ref, wb_ref, wo_ref, fn_ref,
                  x1_ref, h2_ref):
    ya = _dot(hg_ref[...], wa_ref[...])
    yb = _dot(ob_ref[...], wb_ref[...])
    mix = _sigmoid(ga_ref[...]) * ya + _sigmoid(gb_ref[...]) * yb
    x1 = x_ref[...] + _dot(mix.astype(BF16), wo_ref[...])
    x1_ref[...] = x1
    h2 = x1 * lax.rsqrt(jnp.mean(x1 * x1, axis=-1, keepdims=True) + EPS) * fn_ref[...]
    h2_ref[...] = h2.astype(BF16)


def _merge(x2, hg, ob, proj, w_a, w_b, w_o, ffn_norm, tm):
    s, d = x2.shape
    gblk = COL_GATE // d
    row = lambda i: (i, 0)
    return pl.pallas_call(
        _merge_kernel,
        out_shape=[jax.ShapeDtypeStruct((s, d), F32), jax.ShapeDtypeStruct((s, d), BF16)],
        grid=(s // tm,),
        in_specs=[
            pl.BlockSpec((tm, d), row),
            pl.BlockSpec((tm, A_WIDTH), row),
            pl.BlockSpec((tm, B_WIDTH), row),
            pl.BlockSpec((tm, d), lambda i: (i, gblk)),
            pl.BlockSpec((tm, d), lambda i: (i, gblk + 1)),
            _const_spec((A_WIDTH, d)),
            _const_spec((B_WIDTH, d)),
            _const_spec((d, d)),
            _const_spec((1, d)),
        ],
        out_specs=[pl.BlockSpec((tm, d), row), pl.BlockSpec((tm, d), row)],
        compiler_params=_cparams(("arbitrary",)),
        name="merge",
    )(x2, hg, ob, proj, proj, w_a, w_b, w_o, ffn_norm)


def _ffn_kernel(x1_ref, h2_ref, wup_ref, cw_ref, cb_ref, wdn_ref, fin_ref, o_ref,
                tail_scr, u_scr, act_scr, *, cbw):
    tm = h2_ref.shape[0]
    dff = wdn_ref.shape[0]
    halo = 8

    @pl.when(pl.program_id(0) == 0)
    def _():
        tail_scr[...] = jnp.zeros_like(tail_scr)

    h2 = h2_ref[...]

    def conv_cols(c0):
        u = _dot(h2, wup_ref[:, c0:c0 + cbw])
        u_scr[0:halo, :] = tail_scr[:, c0:c0 + cbw]
        u_scr[halo:, :] = u
        tail_scr[:, c0:c0 + cbw] = u[tm - halo:, :]
        cw = cw_ref[:, c0:c0 + cbw]
        return (cw[2:3] * u + cw[1:2] * u_scr[halo - 1:halo - 1 + tm, :]
                + cw[0:1] * u_scr[halo - 2:halo - 2 + tm, :] + cb_ref[:, c0:c0 + cbw])

    for j in range(dff // cbw):
        a = conv_cols(j * cbw)
        v = conv_cols(dff + j * cbw)
        act_scr[:, j * cbw:(j + 1) * cbw] = (a * _sigmoid(a) * v).astype(BF16)

    y = x1_ref[...] + _dot(act_scr[...], wdn_ref[...])
    o_ref[...] = y * lax.rsqrt(jnp.mean(y * y, axis=-1, keepdims=True) + EPS) * fin_ref[...]


def _ffn(x1, h2, w_up, conv_w, conv_b, w_down, final_norm, tm, cbw):
    s, d = x1.shape
    dff = w_down.shape[0]
    row = lambda i: (i, 0)
    return pl.pallas_call(
        functools.partial(_ffn_kernel, cbw=cbw),
        out_shape=jax.ShapeDtypeStruct((s, d), F32),
        grid=(s // tm,),
        in_specs=[
            pl.BlockSpec((tm, d), row),
            pl.BlockSpec((tm, d), row),
            _const_spec((d, 2 * dff)),
            _const_spec((CONV_WIDTH, 2 * dff)),
            _const_spec((1, 2 * dff)),
            _const_spec((dff, d)),
            _const_spec((1, d)),
        ],
        out_specs=pl.BlockSpec((tm, d), row),
        scratch_shapes=[
            pltpu.VMEM((8, 2 * dff), F32),
            pltpu.VMEM((tm + 8, cbw), F32),
            pltpu.VMEM((tm, dff), BF16),
        ],
        compiler_params=_cparams(("arbitrary",)),
        name="ffn",
    )(x1, h2, w_up, conv_w, conv_b, w_down, final_norm)


def _row_tile(s, want):
    t = min(want, s)
    while s % t:
        t //= 2
    return t


def kernel(x, attn_norm, w_in, lower_bounds, hgrn_out_norm, q_norm, kv_norm, w_uq, w_uk, w_uv, w_iq,
           rel_bias, w_branch_a, w_branch_b, w_o, ffn_norm, w_up, conv_w, conv_b, w_down, final_norm):
    bsz, s, d = x.shape
    assert bsz == 1 and attn_norm.shape[0] == 1 and lower_bounds.shape[0] == 2
    x2 = x.reshape(s, d)
    top_k = min(TOPK_MAX, s // 4)

    wi = w_in[0]
    n_small = Q_RANK + KV_RANK + IDX_DIM + IDX_HEADS
    w_pad = jnp.concatenate(
        [wi[:, :COL_GATE].astype(BF16), wi[:, COL_GATE + n_small:].astype(BF16),
         wi[:, COL_GATE:COL_GATE + n_small].astype(BF16), jnp.zeros((d, SMALL_W - n_small), BF16)], axis=1)

    proj = _inproj(x2, attn_norm, w_pad, _row_tile(s, 1024), N_PROJ // 4)

    hg = _hgrn(proj, lower_bounds, hgrn_out_norm, _row_tile(s, 512), A_HEADS)

    qlat, qidxt, widxt, ckvx, ckvt, kidx = _mla_prep(
        proj, q_norm, kv_norm, w_uq[0].astype(BF16), jnp.swapaxes(w_uk[0], 1, 2).astype(BF16),
        w_iq[0].astype(BF16), _row_tile(s, 512))

    t_loc = jnp.arange(TQ, dtype=jnp.int32)[None, :, None]
    s_loc = jnp.arange(TKN, dtype=jnp.int32)[None, None, :]
    off = jnp.array([0, -TQ], jnp.int32)[:, None, None]
    buckets = _t5_bucket(s_loc - t_loc + off)
    biasd = _bias_tab(buckets, rel_bias)

    ob = _dsa(qlat, qidxt, widxt, kidx, ckvt, ckvx, biasd, w_uv[0].astype(BF16), top_k)

    x1, h2 = _merge(x2, hg, ob, proj, w_branch_a[0].astype(BF16), w_branch_b[0].astype(BF16),
                    w_o[0].astype(BF16), ffn_norm, _row_tile(s, 512))

    out = _ffn(x1, h2, w_up[0].astype(BF16), conv_w[0], conv_b, w_down[0].astype(BF16),
               final_norm.reshape(1, d), _row_tile(s, 512), 256)
    return out.reshape(bsz, s, d)
```

```python
import functools
import math

import jax
import jax.numpy as jnp
from jax import lax
from jax.experimental import pallas as pl
from jax.experimental.pallas import tpu as pltpu

F32 = jnp.float32
BF16 = jnp.bfloat16
I32 = jnp.int32

EPS = 1e-6
CHUNK = 64
A_HEADS = 8
A_HEAD_DIM = 128
A_WIDTH = A_HEADS * A_HEAD_DIM
B_HEADS = 16
B_QK_DIM = 64
B_V_DIM = 64
B_WIDTH = B_HEADS * B_V_DIM
Q_RANK = 256
KV_RANK = 128
IDX_HEADS = 8
IDX_DIM = 64
TOPK_MAX = 256
N_BUCKETS = 32
MAX_DISTANCE = 128
CONV_WIDTH = 3

LANES = 128
SUBLANES = 8
VMEM_LIMIT = 56 * 1024 * 1024

COL_QFIG = 0
COL_GATE = 4 * A_WIDTH
COL_SMALL = COL_GATE + 2048
SMALL_W = 512
N_PROJ = COL_SMALL + SMALL_W

TQ = 128
TKA = 512
TK = 512
TKN = 2 * TQ
FAR_UNROLL = 2
NEG = -1e30
EXP_SAFE = 80.0
MAX_BISECT = 1024
LOG2E = 1.4426950408889634


def _cparams(sem):
    return pltpu.CompilerParams(dimension_semantics=sem, vmem_limit_bytes=VMEM_LIMIT)


def _const_spec(shape):
    nd = len(shape)
    return pl.BlockSpec(shape, lambda *_: (0,) * nd, pipeline_mode=pl.Buffered(1))


def _dot(a, b):
    return jnp.dot(a, b, preferred_element_type=F32)


def _dot_nt(a, b):
    return lax.dot_general(a, b, (((1,), (1,)), ((), ())), preferred_element_type=F32)


def _dot_tn(a, b):
    return lax.dot_general(a, b, (((0,), (0,)), ((), ())), preferred_element_type=F32)


def _sigmoid(x):
    return jax.nn.sigmoid(x)


def _inproj_kernel(x_ref, g_ref, w_ref, o_ref, h_scr):
    @pl.when(pl.program_id(1) == 0)
    def _():
        x = x_ref[...]
        y = x * lax.rsqrt(jnp.mean(x * x, axis=-1, keepdims=True) + EPS) * g_ref[...]
        h_scr[...] = y.astype(BF16)

    o_ref[...] = _dot(h_scr[...], w_ref[...])


def _inproj(x2, gain, w_pad, tm, tn):
    s, d = x2.shape
    n = w_pad.shape[1]
    return pl.pallas_call(
        _inproj_kernel,
        out_shape=jax.ShapeDtypeStruct((s, n), F32),
        grid=(s // tm, n // tn),
        in_specs=[
            pl.BlockSpec((tm, d), lambda i, j: (i, 0)),
            pl.BlockSpec((1, d), lambda i, j: (0, 0)),
            pl.BlockSpec((d, tn), lambda i, j: (0, j)),
        ],
        out_specs=pl.BlockSpec((tm, tn), lambda i, j: (i, j)),
        scratch_shapes=[pltpu.VMEM((tm, d), BF16)],
        compiler_params=_cparams(("arbitrary", "arbitrary")),
        name="inproj",
    )(x2, gain, w_pad)


def _hgrn_kernel(q_ref, f_ref, i_ref, g_ref, lbp_ref, on_ref, o_ref,
                 st_ref, b_scr, k_scr, qs_scr, att_scr, *, nch):
    c_ = CHUNK
    half = c_ // 2

    @pl.when(pl.program_id(1) == 0)
    def _():
        st_ref[...] = jnp.zeros_like(st_ref)

    l0 = lbp_ref[0:1, :]
    l1 = lbp_ref[1:2, :]
    lm = jnp.maximum(l0, l1)
    e0 = jnp.exp(l0 - lm)
    e1 = jnp.exp(l1 - lm)
    lb = e0 / (e0 + e1)

    row = lax.broadcasted_iota(I32, (c_, c_), 0)
    col = lax.broadcasted_iota(I32, (c_, c_), 1)
    tri = row >= col
    tri_bf = jnp.where(tri, 1.0, 0.0).astype(BF16)

    q = q_ref[...]
    qs_scr[...] = q * _sigmoid(q)
    fg = lb + (1.0 - lb) * _sigmoid(f_ref[...])
    k_scr[...] = 1.0 - fg
    lg = jnp.log(fg)

    d = A_HEAD_DIM
    hpg = q_ref.shape[1] // d
    heads = [slice(hh * d, (hh + 1) * d) for hh in range(hpg)]

    dev = jnp.zeros((1, 1), F32)
    for c in range(nch):
        l = lg[c * c_:(c + 1) * c_]
        hi = l.astype(BF16)
        r1 = l - hi.astype(F32)
        mid = r1.astype(BF16)
        lo = (r1 - mid.astype(F32)).astype(BF16)
        b = _dot(tri_bf, hi) + _dot(tri_bf, mid) + _dot(tri_bf, lo)
        b_scr[c * c_:(c + 1) * c_, :] = b
        rho = b[half - 1:half, :]
        dev = jnp.maximum(dev, jnp.max(jnp.abs(b - rho), keepdims=True))
    safe = dev[0, 0] <= EXP_SAFE

    @pl.when(safe)
    def _():
        for c in range(nch):
            sl = slice(c * c_, (c + 1) * c_)
            b = b_scr[sl, :]
            rho = b[half - 1:half, :]
            qt = (qs_scr[sl, :] * jnp.exp(b - rho)).astype(BF16)
            kt = (k_scr[sl, :] * jnp.exp(rho - b)).astype(BF16)
            for hh, hs in enumerate(heads):
                att_scr[hh, c] = jnp.where(tri, _dot_nt(qt[:, hs], kt[:, hs]), 0.0)

    @pl.when(jnp.logical_not(safe))
    def _():
        trow = lax.broadcasted_iota(I32, (c_, d), 0)
        for c in range(nch):
            sl = slice(c * c_, (c + 1) * c_)
            for hh, hs in enumerate(heads):
                b = b_scr[sl, hs]
                qv = qs_scr[sl, hs]

                def body(s, att):
                    bs = b_scr[pl.ds(c * c_ + s, 1), :][:, hs]
                    ks = k_scr[pl.ds(c * c_ + s, 1), :][:, hs]
                    dd = jnp.where(trow >= s, b - bs, -jnp.inf)
                    colv = jnp.sum(qv * ks * jnp.exp(dd), axis=-1, keepdims=True)
                    return jnp.where(col == s, colv, att)

                att_scr[hh, c] = lax.fori_loop(0, c_, body, jnp.zeros((c_, c_), F32))

    on = on_ref[...]
    for c in range(nch):
        sl = slice(c * c_, (c + 1) * c_)
        b = b_scr[sl, :]
        bend = b[c_ - 1:c_, :]
        v = i_ref[sl, :].astype(BF16)
        qe = (qs_scr[sl, :] * jnp.exp(b)).astype(BF16)
        kd = (k_scr[sl, :] * jnp.exp(bend - b)).astype(BF16)
        decay = jnp.exp(bend)
        g = g_ref[sl, :]
        gate = g * _sigmoid(g)
        for hh, hs in enumerate(heads):
            st = st_ref[hh]
            o = _dot_nt(qe[:, hs], st.astype(BF16)) + _dot(att_scr[hh, c].astype(BF16), v[:, hs])
            st_ref[hh] = st * decay[:, hs] + _dot_tn(v[:, hs], kd[:, hs])
            o = o * lax.rsqrt(jnp.mean(o * o, axis=-1, keepdims=True) + EPS) * on
            o_ref[sl, hs] = (o * gate[:, hs]).astype(o_ref.dtype)


def _hgrn(proj, lower_bounds, out_norm, t_rows, hpg):
    s = proj.shape[0]
    nch = t_rows // CHUNK
    d = A_HEAD_DIM
    w = hpg * d
    nb = COL_QFIG // w

    def colspec(off):
        return pl.BlockSpec((t_rows, w), lambda h, i, off=off: (i, nb + off * (A_HEADS // hpg) + h))

    return pl.pallas_call(
        functools.partial(_hgrn_kernel, nch=nch),
        out_shape=jax.ShapeDtypeStruct((s, A_WIDTH), BF16),
        grid=(A_HEADS // hpg, s // t_rows),
        in_specs=[
            colspec(0), colspec(1), colspec(2), colspec(3),
            pl.BlockSpec((2, w), lambda h, i: (0, h)),
            pl.BlockSpec((1, d), lambda h, i: (0, 0)),
        ],
        out_specs=pl.BlockSpec((t_rows, w), lambda h, i: (i, h)),
        scratch_shapes=[
            pltpu.VMEM((hpg, d, d), F32),
            pltpu.VMEM((t_rows, w), F32),
            pltpu.VMEM((t_rows, w), F32),
            pltpu.VMEM((t_rows, w), F32),
            pltpu.VMEM((hpg, nch, CHUNK, CHUNK), F32),
        ],
        compiler_params=_cparams(("arbitrary", "arbitrary")),
        name="hgrn",
    )(proj, proj, proj, proj, lower_bounds, out_norm)


def _mla_prep_kernel(p_ref, qn_ref, kn_ref, wuq_ref, wukt_ref, wiq_ref,
                     qlat_ref, qidxt_ref, widxt_ref, ckvx_ref, ckvt_ref, kidx_ref):
    p = p_ref[...]
    cq = p[:, 0:Q_RANK]
    cq = cq * lax.rsqrt(jnp.mean(cq * cq, axis=-1, keepdims=True) + EPS) * qn_ref[...]
    cqb = cq.astype(BF16)
    ckv = p[:, Q_RANK:Q_RANK + KV_RANK]
    ckv = ckv * lax.rsqrt(jnp.mean(ckv * ckv, axis=-1, keepdims=True) + EPS) * kn_ref[...]

    q = _dot(cqb, wuq_ref[...])
    scale = B_QK_DIM ** -0.5 * LOG2E
    for h in range(B_HEADS):
        qh = q[:, h * B_QK_DIM:(h + 1) * B_QK_DIM].astype(BF16)
        qlat_ref[:, h * KV_RANK:(h + 1) * KV_RANK] = (_dot(qh, wukt_ref[h]) * scale).astype(BF16)
    qidxt_ref[...] = _dot(cqb, wiq_ref[...]).T.astype(BF16)

    rest = p[:, Q_RANK + KV_RANK:]
    widxt_ref[...] = rest.T[IDX_DIM:IDX_DIM + IDX_HEADS, :] * ((IDX_HEADS * IDX_DIM) ** -0.5)
    ckvx_ref[:, 0:KV_RANK] = ckv.astype(BF16)
    ckvx_ref[:, KV_RANK:] = jnp.ones((p.shape[0], KV_RANK), BF16)
    ckvt_ref[...] = ckv.T.astype(BF16)
    kidx_ref[...] = rest[:, 0:IDX_DIM].astype(BF16)


def _mla_prep(proj, q_norm, kv_norm, w_uq, w_ukt, w_iq, tm):
    s = proj.shape[0]
    return pl.pallas_call(
        _mla_prep_kernel,
        out_shape=[
            jax.ShapeDtypeStruct((s, B_HEADS * KV_RANK), BF16),
            jax.ShapeDtypeStruct((IDX_HEADS * IDX_DIM, s), BF16),
            jax.ShapeDtypeStruct((IDX_HEADS, s), F32),
            jax.ShapeDtypeStruct((s, 2 * KV_RANK), BF16),
            jax.ShapeDtypeStruct((KV_RANK, s), BF16),
            jax.ShapeDtypeStruct((s, IDX_DIM), BF16),
        ],
        grid=(s // tm,),
        in_specs=[
            pl.BlockSpec((tm, SMALL_W), lambda i: (i, COL_SMALL // SMALL_W)),
            _const_spec((1, Q_RANK)),
            _const_spec((1, KV_RANK)),
            _const_spec((Q_RANK, B_HEADS * B_QK_DIM)),
            _const_spec((B_HEADS, B_QK_DIM, KV_RANK)),
            _const_spec((Q_RANK, IDX_HEADS * IDX_DIM)),
        ],
        out_specs=[
            pl.BlockSpec((tm, B_HEADS * KV_RANK), lambda i: (i, 0)),
            pl.BlockSpec((IDX_HEADS * IDX_DIM, tm), lambda i: (0, i)),
            pl.BlockSpec((IDX_HEADS, tm), lambda i: (0, i)),
            pl.BlockSpec((tm, 2 * KV_RANK), lambda i: (i, 0)),
            pl.BlockSpec((KV_RANK, tm), lambda i: (0, i)),
            pl.BlockSpec((tm, IDX_DIM), lambda i: (i, 0)),
        ],
        compiler_params=_cparams(("arbitrary",)),
        name="mla_prep",
    )(proj, q_norm, kv_norm, w_uq, w_ukt, w_iq)


def _bias_tab_kernel(bkt_ref, rb_ref, o_ref):
    far = N_BUCKETS // 2 - 1
    for v in range(2):
        bkt = bkt_ref[v]
        for h in range(B_HEADS):
            acc = jnp.zeros(bkt.shape, F32)
            for k in range(N_BUCKETS):
                acc = jnp.where(bkt == k, (rb_ref[k, h] - rb_ref[far, h]) * LOG2E, acc)
            o_ref[v, h] = acc


def _bias_tab(buckets, rel_bias):
    return pl.pallas_call(
        _bias_tab_kernel,
        out_shape=jax.ShapeDtypeStruct((2, B_HEADS, TQ, TKN), F32),
        in_specs=[
            pl.BlockSpec(memory_space=pltpu.VMEM),
            pl.BlockSpec(memory_space=pltpu.SMEM),
        ],
        out_specs=pl.BlockSpec(memory_space=pltpu.VMEM),
        compiler_params=pltpu.CompilerParams(vmem_limit_bytes=VMEM_LIMIT),
        name="bias_tab",
    )(buckets, rel_bias)


def _t5_bucket(rel):
    half = N_BUCKETS // 2
    max_exact = half // 2
    ret = jnp.where(rel > 0, half, 0)
    n = jnp.abs(rel)
    nf = jnp.maximum(n, 1).astype(jnp.float32)
    large = max_exact + (jnp.log(nf / max_exact) / math.log(MAX_DISTANCE / max_exact) * (half - max_exact)).astype(jnp.int32)
    large = jnp.minimum(large, half - 1)
    return ret + jnp.where(n < max_exact, n, large)


def _dsa_kernel(qlat_ref, qidxt_ref, widxt_ref, kidx_ref, ckvt_ref, ckvx_ref, biasd_ref, wuv_ref,
                o_ref, isc_scr, t_scr, lhs_scr, rhs_scr, m_scr, acc_scr, p_scr, al_scr, qp_scr,
                *, top_k):
    qi = pl.program_id(0)
    q0 = qi * TQ
    kf = float(top_k)
    qlane = lax.broadcasted_iota(I32, (1, TQ), 1)
    lim = q0 + (qlane // CHUNK + 1) * CHUNK
    krow = lax.broadcasted_iota(I32, (TKA, TQ), 0)

    def fold(x, op):
        n = x.shape[0] // SUBLANES
        x = x.reshape(n, SUBLANES, TQ)
        while n > 1:
            n //= 2
            x = op(x[:n], x[n:2 * n])
        return x[0]

    def lanes_x(v, width):
        return jnp.concatenate([v] * (width // LANES), axis=1)

    na = (q0 + TQ + TKA - 1) // TKA
    qt = qidxt_ref[...]
    wt = widxt_ref[...]
    wp = []
    for p in range(IDX_HEADS // 2):
        qa = qt[(2 * p) * IDX_DIM:(2 * p + 1) * IDX_DIM, :]
        qb = qt[(2 * p + 1) * IDX_DIM:(2 * p + 2) * IDX_DIM, :]
        qp_scr[p] = jnp.concatenate([qa, qb], axis=1)
        wp.append(jnp.concatenate([wt[2 * p:2 * p + 1, :], wt[2 * p + 1:2 * p + 2, :]], axis=1))

    def scores(k0):
        kb = kidx_ref[pl.ds(k0, TKA), :]
        acc = jnp.zeros((TKA, TQ), F32)
        for p in range(IDX_HEADS // 2):
            s2 = jnp.maximum(_dot(kb, qp_scr[p]), 0.0) * wp[p]
            acc = acc + (s2[:, 0:TQ] + s2[:, TQ:])
        return acc

    def emit(k0, acc, stats, masked):
        mn, mx, c_ge0, c_gt0 = stats
        mn = jnp.minimum(mn, fold(acc, jnp.minimum))
        mx = jnp.maximum(mx, fold(acc, jnp.maximum))
        a = jnp.where((k0 + krow) < lim, acc, -jnp.inf) if masked else acc
        c_ge0 = c_ge0 + fold(jnp.where(a >= 0.0, 1.0, 0.0), jnp.add)
        c_gt0 = c_gt0 + fold(jnp.where(a > 0.0, 1.0, 0.0), jnp.add)
        isc_scr[pl.ds(k0, TKA), :] = a
        return mn, mx, c_ge0, c_gt0

    def phase_a(jb, stats, masked):
        k0 = pl.multiple_of(jb * (2 * TKA), 2 * TKA)
        a0 = scores(k0)
        a1 = scores(k0 + TKA)
        return emit(k0 + TKA, a1, emit(k0, a0, stats, masked), masked)

    zeros8 = jnp.zeros((SUBLANES, TQ), F32)
    n_trip = (na + 1) // 2
    stats = lax.fori_loop(
        0, n_trip - 1, functools.partial(phase_a, masked=False),
        (jnp.full((SUBLANES, TQ), jnp.inf, F32), jnp.full((SUBLANES, TQ), -jnp.inf, F32), zeros8, zeros8))
    mn, mx, pc_ge0, pc_gt0 = phase_a(n_trip - 1, stats, True)

    def count(pred):
        def body(g, cnt):
            k0 = pl.multiple_of(g * TKA, TKA)
            x = isc_scr[pl.ds(k0, TKA), :]
            return cnt + fold(jnp.where(pred(x, k0 + krow), 1.0, 0.0), jnp.add)
        return jnp.sum(lax.fori_loop(0, na, body, zeros8), axis=0, keepdims=True)

    t_scr[...] = jnp.full(t_scr.shape, -jnp.inf, F32)

    @pl.when(q0 + CHUNK > top_k)
    def _():
        def row_total(part):
            return jnp.sum(part, axis=0, keepdims=True)

        rmin = jnp.min(mn, axis=0, keepdims=True)
        rmax = jnp.max(mx, axis=0, keepdims=True)
        n_adm = lim.astype(F32)
        c_ge0, c_gt0 = row_total(pc_ge0), row_total(pc_gt0)
        above = c_gt0 >= kf
        below = c_ge0 < kf
        tiny = float(jnp.finfo(F32).tiny)
        hi_top = rmax + (rmax - rmin) + jnp.abs(rmax) * 1e-3 + 1e-30
        lo0 = jnp.where(below, rmin, 0.0)
        clo0 = jnp.where(below, n_adm, c_ge0)
        hi0 = jnp.where(below, 0.0, jnp.where(above, hi_top, tiny))
        chi0 = jnp.where(below, c_ge0, jnp.where(above, 0.0, c_gt0))
        log_k = math.log(kf)

        def plan(it, lo, hi, clo, chi, done):
            bis = 0.5 * lo + 0.5 * hi
            act = jnp.where(jnp.logical_and(done == 0.0, jnp.logical_and(bis > lo, bis < hi)), 1.0, 0.0)
            llo = jnp.log(clo)
            lhi = jnp.log(jnp.maximum(chi, 0.5))
            t = jnp.clip((llo - log_k) * pl.reciprocal(llo - lhi, approx=True), 0.03, 0.97)
            guess = lo + t * (hi - lo)
            use = jnp.logical_and(jnp.logical_and(guess > lo, guess < hi), it % 3 != 2)
            return act, jnp.sum(act), jnp.where(use, guess, bis)

        def count_ge(mid):
            def body(g, cnt):
                k0 = pl.multiple_of(g * TKA, TKA)
                return cnt + fold(jnp.where(isc_scr[pl.ds(k0, TKA), :] >= mid, 1.0, 0.0), jnp.add)

            return row_total(lax.fori_loop(0, na, body, zeros8))

        def cond(st):
            it, nact = st[0], st[1]
            return jnp.logical_and(it < MAX_BISECT, nact > 0.0)

        def body(st):
            it, nact, lo, hi, clo, chi, done, act, mid = st
            c = count_ge(mid)
            on = act > 0.0
            up = jnp.logical_and(on, c >= kf)
            dn = jnp.logical_and(on, c < kf)
            lo = jnp.where(up, mid, lo)
            clo = jnp.where(up, c, clo)
            hi = jnp.where(dn, mid, hi)
            chi = jnp.where(dn, c, chi)
            done = jnp.where(jnp.logical_and(on, c == kf), 1.0, done)
            act, nact, mid = plan(it + 1, lo, hi, clo, chi, done)
            return it + 1, nact, lo, hi, clo, chi, done, act, mid

        done0 = jnp.where(c_ge0 == kf, 1.0, 0.0)
        act0, nact0, mid0 = plan(0, lo0, hi0, clo0, chi0, done0)
        st = lax.while_loop(cond, body, (jnp.int32(0), nact0, lo0, hi0, clo0, chi0, done0, act0, mid0))
        t_lo, t_hi, done = st[2], st[3], st[6]
        t_scr[0:1, :] = t_lo

        n_tie = jnp.sum(jnp.where(done == 0.0, 1.0, 0.0))

        @pl.when(n_tie > 0.0)
        def _():
            lo, hi, tie = t_lo, t_hi, done == 0.0
            r = kf - count(lambda x, idx: x >= hi)
            p = jnp.zeros((1, TQ), I32)
            nbits = max(1, (isc_scr.shape[0] - 1).bit_length())
            for bit in range(nbits - 1, -1, -1):
                cand = p + (1 << bit)
                f = count(lambda x, idx: jnp.logical_and(jnp.logical_and(x >= lo, x < hi), idx < cand))
                p = jnp.where(f < r, cand, p)

            def drop(g, carry):
                k0 = pl.multiple_of(g * TKA, TKA)
                x = isc_scr[pl.ds(k0, TKA), :]
                ex = jnp.logical_and(jnp.logical_and(tie, (k0 + krow) > p),
                                     jnp.logical_and(x >= lo, x < hi))
                isc_scr[pl.ds(k0, TKA), :] = jnp.where(ex, -jnp.inf, x)
                return carry

            lax.fori_loop(0, na, drop, 0)

    thr = t_scr[0:1, :]
    for h in range(B_HEADS):
        lhs_scr[h * TQ:(h + 1) * TQ, 0:KV_RANK] = qlat_ref[:, h * KV_RANK:(h + 1) * KV_RANK]

    @pl.when(qi == 0)
    def _():
        eye = jnp.where(lax.broadcasted_iota(I32, (TQ, TQ), 0) == lax.broadcasted_iota(I32, (TQ, TQ), 1),
                        1.0, 0.0).astype(BF16)
        for h in range(B_HEADS):
            lhs_scr[h * TQ:(h + 1) * TQ, KV_RANK:] = eye

    m_scr[...] = jnp.full(m_scr.shape, NEG, F32)
    acc_scr[...] = jnp.zeros_like(acc_scr)

    def accumulate(h, k0_prev, w_prev):
        rs = slice(h * TQ, (h + 1) * TQ)
        kv = ckvx_ref[pl.ds(k0_prev, w_prev), :]
        acc_scr[rs, :] = (acc_scr[rs, :] * lanes_x(al_scr[rs, :], 2 * KV_RANK)
                          + _dot(p_scr[rs, 0:w_prev], kv))

    def stage(k0_prev, w_prev, k0, width, key_lim, with_bias):
        x = isc_scr[pl.ds(k0, width), :]
        sel = jnp.logical_and(x >= thr, (k0 + krow[0:width, :]) < key_lim)
        rhs_scr[0:KV_RANK, 0:width] = ckvt_ref[:, pl.ds(k0, width)]
        rhs_scr[KV_RANK:, 0:width] = jnp.where(sel, 0.0, NEG).T.astype(BF16)
        rhs = rhs_scr[:, 0:width]
        for h in range(B_HEADS):
            rs = slice(h * TQ, (h + 1) * TQ)
            if w_prev:
                accumulate(h, k0_prev, w_prev)
            s = _dot(lhs_scr[rs, :], rhs)
            if with_bias:
                s = s + biasd_ref[0, h]
            m_old = m_scr[rs, :]
            m_new = jnp.maximum(m_old, jnp.max(s, axis=1, keepdims=True))
            al_scr[rs, :] = jnp.exp2(m_old - m_new)
            p_scr[rs, 0:width] = jnp.exp2(s - lanes_x(m_new, width)).astype(BF16)
            m_scr[rs, :] = m_new

    far_lim = jnp.maximum(q0 - TQ, 0)
    nf = (far_lim + TK - 1) // TK

    @pl.when(nf == 0)
    def _():
        p_scr[...] = jnp.zeros_like(p_scr)
        al_scr[...] = jnp.ones_like(al_scr)

    @pl.when(nf > 0)
    def _():
        stage(0, 0, 0, TK, far_lim, False)

    def far_stage(jb):
        stage(pl.multiple_of((jb - 1) * TK, TK), TK, pl.multiple_of(jb * TK, TK), TK, far_lim, False)

    def far_group(jj, carry):
        for u in range(FAR_UNROLL):
            far_stage(FAR_UNROLL * jj + 1 + u)
        return carry

    n_group = jnp.maximum(nf - 1, 0) // FAR_UNROLL
    lax.fori_loop(0, n_group, far_group, 0)

    def far_single(jb, carry):
        far_stage(jb)
        return carry

    lax.fori_loop(n_group * FAR_UNROLL + 1, nf, far_single, 0)

    k0_last = pl.multiple_of(jnp.maximum(nf - 1, 0) * TK, TK)
    k0_near = pl.multiple_of(far_lim, TQ)
    stage(k0_last, TK, k0_near, TKN, lim, True)
    for h in range(B_HEADS):
        accumulate(h, k0_near, TKN)

    for h in range(B_HEADS):
        rs = slice(h * TQ, (h + 1) * TQ)
        a = acc_scr[rs, :]
        olat = (a[:, 0:KV_RANK] / a[:, KV_RANK:]).astype(BF16)
        o_ref[:, h * B_V_DIM:(h + 1) * B_V_DIM] = _dot(olat, wuv_ref[h]).astype(o_ref.dtype)


def _dsa(qlat, qidxt, widxt, kidx, ckvt, ckvx, biasd, w_uv, top_k):
    s = qlat.shape[0]
    assert s % (2 * TKA) == 0 and TK == TKA and top_k % TQ == 0
    return pl.pallas_call(
        functools.partial(_dsa_kernel, top_k=top_k),
        out_shape=jax.ShapeDtypeStruct((s, B_WIDTH), BF16),
        grid=(s // TQ,),
        in_specs=[
            pl.BlockSpec((TQ, B_HEADS * KV_RANK), lambda i: (i, 0)),
            pl.BlockSpec((IDX_HEADS * IDX_DIM, TQ), lambda i: (0, i)),
            pl.BlockSpec((IDX_HEADS, TQ), lambda i: (0, i)),
            _const_spec((s, IDX_DIM)),
            _const_spec((KV_RANK, s)),
            _const_spec((s, 2 * KV_RANK)),
            pl.BlockSpec((1, B_HEADS, TQ, TKN), lambda i: (jnp.minimum(i, 1), 0, 0, 0)),
            _const_spec((B_HEADS, KV_RANK, B_V_DIM)),
        ],
        out_specs=pl.BlockSpec((TQ, B_WIDTH), lambda i: (i, 0)),
        scratch_shapes=[
            pltpu.VMEM((s, TQ), F32),
            pltpu.VMEM((SUBLANES, TQ), F32),
            pltpu.VMEM((B_HEADS * TQ, 2 * KV_RANK), BF16),
            pltpu.VMEM((2 * KV_RANK, TK), BF16),
            pltpu.VMEM((B_HEADS * TQ, LANES), F32),
            pltpu.VMEM((B_HEADS * TQ, 2 * KV_RANK), F32),
            pltpu.VMEM((B_HEADS * TQ, TK), BF16),
            pltpu.VMEM((B_HEADS * TQ, LANES), F32),
            pltpu.VMEM((IDX_HEADS // 2, IDX_DIM, 2 * TQ), BF16),
        ],
        compiler_params=_cparams(("arbitrary",)),
        name="dsa",
    )(qlat, qidxt, widxt, kidx, ckvt, ckvx, biasd, w_uv)


def _merge_kernel(x_ref, hg_ref, ob_ref, ga_ref, gb_ref, wa_ref, wb_ref, wo_ref, fn_ref,
                  x1_ref, h2_ref):
    ya = _dot(hg_ref[...], wa_ref[...])
    yb = _dot(ob_ref[...], wb_ref[...])
    mix = _sigmoid(ga_ref[...]) * ya + _sigmoid(gb_ref[...]) * yb
    x1 = x_ref[...] + _dot(mix.astype(BF16), wo_ref[...])
    x1_ref[...] = x1
    h2 = x1 * lax.rsqrt(jnp.mean(x1 * x1, axis=-1, keepdims=True) + EPS) * fn_ref[...]
    h2_ref[...] = h2.astype(BF16)


def _merge(x2, hg, ob, proj, w_a, w_b, w_o, ffn_norm, tm):
    s, d = x2.shape
    gblk = COL_GATE // d
    row = lambda i: (i, 0)
    return pl.pallas_call(
        _merge_kernel,
        out_shape=[jax.ShapeDtypeStruct((s, d), F32), jax.ShapeDtypeStruct((s, d), BF16)],
        grid=(s // tm,),
        in_specs=[
            pl.BlockSpec((tm, d), row),
            pl.BlockSpec((tm, A_WIDTH), row),
            pl.BlockSpec((tm, B_WIDTH), row),
            pl.BlockSpec((tm, d), lambda i: (i, gblk)),
            pl.BlockSpec((tm, d), lambda i: (i, gblk + 1)),
            _const_spec((A_WIDTH, d)),
            _const_spec((B_WIDTH, d)),
            _const_spec((d, d)),
            _const_spec((1, d)),
        ],
        out_specs=[pl.BlockSpec((tm, d), row), pl.BlockSpec((tm, d), row)],
        compiler_params=_cparams(("arbitrary",)),
        name="merge",
    )(x2, hg, ob, proj, proj, w_a, w_b, w_o, ffn_norm)


def _ffn_kernel(x1_ref, h2_ref, wup_ref, cw_ref, cb_ref, wdn_ref, fin_ref, o_ref,
                tail_scr, u_scr, act_scr, *, cbw):
    tm = h2_ref.shape[0]
    dff = wdn_ref.shape[0]
    halo = SUBLANES

    @pl.when(pl.program_id(0) == 0)
    def _():
        tail_scr[...] = jnp.zeros_like(tail_scr)

    h2 = h2_ref[...]

    def conv_cols(c0):
        u = _dot(h2, wup_ref[:, c0:c0 + cbw])
        u_scr[0:halo, :] = tail_scr[:, c0:c0 + cbw]
        u_scr[halo:, :] = u
        tail_scr[:, c0:c0 + cbw] = u[tm - halo:, :]
        cw = cw_ref[:, c0:c0 + cbw]
        return (cw[2:3] * u + cw[1:2] * u_scr[halo - 1:halo - 1 + tm, :]
                + cw[0:1] * u_scr[halo - 2:halo - 2 + tm, :] + cb_ref[:, c0:c0 + cbw])

    for j in range(dff // cbw):
        a = conv_cols(j * cbw)
        v = conv_cols(dff + j * cbw)
        act_scr[:, j * cbw:(j + 1) * cbw] = (a * _sigmoid(a) * v).astype(BF16)

    y = x1_ref[...] + _dot(act_scr[...], wdn_ref[...])
    o_ref[...] = y * lax.rsqrt(jnp.mean(y * y, axis=-1, keepdims=True) + EPS) * fin_ref[...]


def _ffn(x1, h2, w_up, conv_w, conv_b, w_down, final_norm, tm, cbw):
    s, d = x1.shape
    dff = w_down.shape[0]
    row = lambda i: (i, 0)
    return pl.pallas_call(
        functools.partial(_ffn_kernel, cbw=cbw),
        out_shape=jax.ShapeDtypeStruct((s, d), F32),
        grid=(s // tm,),
        in_specs=[
            pl.BlockSpec((tm, d), row),
            pl.BlockSpec((tm, d), row),
            _const_spec((d, 2 * dff)),
            _const_spec((CONV_WIDTH, 2 * dff)),
            _const_spec((1, 2 * dff)),
            _const_spec((dff, d)),
            _const_spec((1, d)),
        ],
        out_specs=pl.BlockSpec((tm, d), row),
        scratch_shapes=[
            pltpu.VMEM((SUBLANES, 2 * dff), F32),
            pltpu.VMEM((tm + SUBLANES, cbw), F32),
            pltpu.VMEM((tm, dff), BF16),
        ],
        compiler_params=_cparams(("arbitrary",)),
        name="ffn",
    )(x1, h2, w_up, conv_w, conv_b, w_down, final_norm)


def _row_tile(s, want):
    t = min(want, s)
    while s % t:
        t //= 2
    return t


def _tiles(s):
    return dict(
        inproj_rows=_row_tile(s, 1024),
        inproj_cols=N_PROJ // 4,
        hgrn_rows=_row_tile(s, 512),
        prep_rows=_row_tile(s, 512),
        merge_rows=_row_tile(s, 512),
        ffn_rows=_row_tile(s, 512),
        ffn_cols=256,
    )


def kernel(x, attn_norm, w_in, lower_bounds, hgrn_out_norm, q_norm, kv_norm, w_uq, w_uk, w_uv, w_iq,
           rel_bias, w_branch_a, w_branch_b, w_o, ffn_norm, w_up, conv_w, conv_b, w_down, final_norm):
    bsz, s, d = x.shape
    assert bsz == 1 and attn_norm.shape[0] == 1 and lower_bounds.shape[0] == 2
    x2 = x.reshape(s, d)
    top_k = min(TOPK_MAX, s // 4)

    wi = w_in[0]
    n_small = Q_RANK + KV_RANK + IDX_DIM + IDX_HEADS
    w_pad = jnp.concatenate(
        [wi[:, :COL_GATE].astype(BF16), wi[:, COL_GATE + n_small:].astype(BF16),
         wi[:, COL_GATE:COL_GATE + n_small].astype(BF16), jnp.zeros((d, SMALL_W - n_small), BF16)], axis=1)

    t = _tiles(s)
    proj = _inproj(x2, attn_norm, w_pad, t["inproj_rows"], t["inproj_cols"])

    hg = _hgrn(proj, lower_bounds, hgrn_out_norm, t["hgrn_rows"], A_HEADS)

    qlat, qidxt, widxt, ckvx, ckvt, kidx = _mla_prep(
        proj, q_norm, kv_norm, w_uq[0].astype(BF16), jnp.swapaxes(w_uk[0], 1, 2).astype(BF16),
        w_iq[0].astype(BF16), t["prep_rows"])

    t_loc = jnp.arange(TQ, dtype=jnp.int32)[None, :, None]
    s_loc = jnp.arange(TKN, dtype=jnp.int32)[None, None, :]
    off = jnp.array([0, -TQ], jnp.int32)[:, None, None]
    buckets = _t5_bucket(s_loc - t_loc + off)
    biasd = _bias_tab(buckets, rel_bias)

    ob = _dsa(qlat, qidxt, widxt, kidx, ckvt, ckvx, biasd, w_uv[0].astype(BF16), top_k)

    x1, h2 = _merge(x2, hg, ob, proj, w_branch_a[0].astype(BF16), w_branch_b[0].astype(BF16),
                    w_o[0].astype(BF16), ffn_norm, t["merge_rows"])

    out = _ffn(x1, h2, w_up[0].astype(BF16), conv_w[0], conv_b, w_down[0].astype(BF16),
               final_norm.reshape(1, d), t["ffn_rows"], t["ffn_cols"])
    return out.reshape(bsz, s, d)
```

```python
import functools
import math

import jax
import jax.numpy as jnp
from jax import lax
from jax.experimental import pallas as pl
from jax.experimental.pallas import tpu as pltpu

F32 = jnp.float32
BF16 = jnp.bfloat16
I32 = jnp.int32

EPS = 1e-6
CHUNK = 64
A_HEADS = 8
A_HEAD_DIM = 128
A_WIDTH = A_HEADS * A_HEAD_DIM
B_HEADS = 16
B_QK_DIM = 64
B_V_DIM = 64
B_WIDTH = B_HEADS * B_V_DIM
Q_RANK = 256
KV_RANK = 128
IDX_HEADS = 8
IDX_DIM = 64
TOPK_MAX = 256
N_BUCKETS = 32
MAX_DISTANCE = 128
CONV_WIDTH = 3

LANES = 128
SUBLANES = 8
VMEM_LIMIT = 56 * 1024 * 1024

COL_QFIG = 0
COL_GATE = 4 * A_WIDTH
COL_SMALL = COL_GATE + 2048
SMALL_W = 512
N_PROJ = COL_SMALL + SMALL_W

TQ = 128
TKA = 512
TK = 512
TKN = 2 * TQ
FAR_UNROLL = 2
NEG = -1e30
EXP_SAFE = 80.0
MAX_BISECT = 1024
LOG2E = 1.4426950408889634


def _cparams(sem):
    return pltpu.CompilerParams(dimension_semantics=sem, vmem_limit_bytes=VMEM_LIMIT)


def _const_spec(shape):
    nd = len(shape)
    return pl.BlockSpec(shape, lambda *_: (0,) * nd, pipeline_mode=pl.Buffered(1))


def _dot(a, b):
    return jnp.dot(a, b, preferred_element_type=F32)


def _dot_nt(a, b):
    return lax.dot_general(a, b, (((1,), (1,)), ((), ())), preferred_element_type=F32)


def _dot_tn(a, b):
    return lax.dot_general(a, b, (((0,), (0,)), ((), ())), preferred_element_type=F32)


def _sigmoid(x):
    return jax.nn.sigmoid(x)


def _inproj_kernel(x_ref, g_ref, w_ref, o_ref, h_scr):
    @pl.when(pl.program_id(1) == 0)
    def _():
        x = x_ref[...]
        y = x * lax.rsqrt(jnp.mean(x * x, axis=-1, keepdims=True) + EPS) * g_ref[...]
        h_scr[...] = y.astype(BF16)

    o_ref[...] = _dot(h_scr[...], w_ref[...])


def _inproj(x2, gain, w_pad, tm, tn):
    s, d = x2.shape
    n = w_pad.shape[1]
    return pl.pallas_call(
        _inproj_kernel,
        out_shape=jax.ShapeDtypeStruct((s, n), F32),
        grid=(s // tm, n // tn),
        in_specs=[
            pl.BlockSpec((tm, d), lambda i, j: (i, 0)),
            pl.BlockSpec((1, d), lambda i, j: (0, 0)),
            pl.BlockSpec((d, tn), lambda i, j: (0, j)),
        ],
        out_specs=pl.BlockSpec((tm, tn), lambda i, j: (i, j)),
        scratch_shapes=[pltpu.VMEM((tm, d), BF16)],
        compiler_params=_cparams(("arbitrary", "arbitrary")),
        name="inproj",
    )(x2, gain, w_pad)


def _hgrn_kernel(q_ref, f_ref, i_ref, g_ref, lbp_ref, on_ref, o_ref,
                 st_ref, b_scr, k_scr, qs_scr, att_scr, *, nch):
    c_ = CHUNK
    half = c_ // 2

    @pl.when(pl.program_id(1) == 0)
    def _():
        st_ref[...] = jnp.zeros_like(st_ref)

    l0 = lbp_ref[0:1, :]
    l1 = lbp_ref[1:2, :]
    lm = jnp.maximum(l0, l1)
    e0 = jnp.exp(l0 - lm)
    e1 = jnp.exp(l1 - lm)
    lb = e0 / (e0 + e1)

    row = lax.broadcasted_iota(I32, (c_, c_), 0)
    col = lax.broadcasted_iota(I32, (c_, c_), 1)
    tri = row >= col
    tri_bf = jnp.where(tri, 1.0, 0.0).astype(BF16)

    q = q_ref[...]
    qs_scr[...] = q * _sigmoid(q)
    fg = lb + (1.0 - lb) * _sigmoid(f_ref[...])
    k_scr[...] = 1.0 - fg
    lg = jnp.log(fg)

    d = A_HEAD_DIM
    hpg = q_ref.shape[1] // d
    heads = [slice(hh * d, (hh + 1) * d) for hh in range(hpg)]

    dev = jnp.zeros((1, 1), F32)
    for c in range(nch):
        l = lg[c * c_:(c + 1) * c_]
        hi = l.astype(BF16)
        r1 = l - hi.astype(F32)
        mid = r1.astype(BF16)
        lo = (r1 - mid.astype(F32)).astype(BF16)
        b = _dot(tri_bf, hi) + _dot(tri_bf, mid) + _dot(tri_bf, lo)
        b_scr[c * c_:(c + 1) * c_, :] = b
        rho = b[half - 1:half, :]
        dev = jnp.maximum(dev, jnp.max(jnp.abs(b - rho), keepdims=True))
    safe = dev[0, 0] <= EXP_SAFE

    @pl.when(safe)
    def _():
        for c in range(nch):
            sl = slice(c * c_, (c + 1) * c_)
            b = b_scr[sl, :]
            rho = b[half - 1:half, :]
            qt = (qs_scr[sl, :] * jnp.exp(b - rho)).astype(BF16)
            kt = (k_scr[sl, :] * jnp.exp(rho - b)).astype(BF16)
            for hh, hs in enumerate(heads):
                att_scr[hh, c] = jnp.where(tri, _dot_nt(qt[:, hs], kt[:, hs]), 0.0)

    @pl.when(jnp.logical_not(safe))
    def _():
        trow = lax.broadcasted_iota(I32, (c_, d), 0)
        for c in range(nch):
            sl = slice(c * c_, (c + 1) * c_)
            for hh, hs in enumerate(heads):
                b = b_scr[sl, hs]
                qv = qs_scr[sl, hs]

                def body(s, att):
                    bs = b_scr[pl.ds(c * c_ + s, 1), :][:, hs]
                    ks = k_scr[pl.ds(c * c_ + s, 1), :][:, hs]
                    dd = jnp.where(trow >= s, b - bs, -jnp.inf)
                    colv = jnp.sum(qv * ks * jnp.exp(dd), axis=-1, keepdims=True)
                    return jnp.where(col == s, colv, att)

                att_scr[hh, c] = lax.fori_loop(0, c_, body, jnp.zeros((c_, c_), F32))

    on = on_ref[...]
    for c in range(nch):
        sl = slice(c * c_, (c + 1) * c_)
        b = b_scr[sl, :]
        bend = b[c_ - 1:c_, :]
        v = i_ref[sl, :].astype(BF16)
        qe = (qs_scr[sl, :] * jnp.exp(b)).astype(BF16)
        kd = (k_scr[sl, :] * jnp.exp(bend - b)).astype(BF16)
        decay = jnp.exp(bend)
        g = g_ref[sl, :]
        gate = g * _sigmoid(g)
        for hh, hs in enumerate(heads):
            st = st_ref[hh]
            o = _dot_nt(qe[:, hs], st.astype(BF16)) + _dot(att_scr[hh, c].astype(BF16), v[:, hs])
            st_ref[hh] = st * decay[:, hs] + _dot_tn(v[:, hs], kd[:, hs])
            o = o * lax.rsqrt(jnp.mean(o * o, axis=-1, keepdims=True) + EPS) * on
            o_ref[sl, hs] = (o * gate[:, hs]).astype(o_ref.dtype)


def _hgrn(proj, lower_bounds, out_norm, t_rows, hpg):
    s = proj.shape[0]
    nch = t_rows // CHUNK
    d = A_HEAD_DIM
    w = hpg * d
    nb = COL_QFIG // w

    def colspec(off):
        return pl.BlockSpec((t_rows, w), lambda h, i, off=off: (i, nb + off * (A_HEADS // hpg) + h))

    return pl.pallas_call(
        functools.partial(_hgrn_kernel, nch=nch),
        out_shape=jax.ShapeDtypeStruct((s, A_WIDTH), BF16),
        grid=(A_HEADS // hpg, s // t_rows),
        in_specs=[
            colspec(0), colspec(1), colspec(2), colspec(3),
            pl.BlockSpec((2, w), lambda h, i: (0, h)),
            pl.BlockSpec((1, d), lambda h, i: (0, 0)),
        ],
        out_specs=pl.BlockSpec((t_rows, w), lambda h, i: (i, h)),
        scratch_shapes=[
            pltpu.VMEM((hpg, d, d), F32),
            pltpu.VMEM((t_rows, w), F32),
            pltpu.VMEM((t_rows, w), F32),
            pltpu.VMEM((t_rows, w), F32),
            pltpu.VMEM((hpg, nch, CHUNK, CHUNK), F32),
        ],
        compiler_params=_cparams(("arbitrary", "arbitrary")),
        name="hgrn",
    )(proj, proj, proj, proj, lower_bounds, out_norm)


def _mla_prep_kernel(p_ref, qn_ref, kn_ref, wuq_ref, wukt_ref, wiq_ref,
                     qlat_ref, qidxt_ref, widxt_ref, ckvx_ref, ckvt_ref, kidx_ref):
    p = p_ref[...]
    cq = p[:, 0:Q_RANK]
    cq = cq * lax.rsqrt(jnp.mean(cq * cq, axis=-1, keepdims=True) + EPS) * qn_ref[...]
    cqb = cq.astype(BF16)
    ckv = p[:, Q_RANK:Q_RANK + KV_RANK]
    ckv = ckv * lax.rsqrt(jnp.mean(ckv * ckv, axis=-1, keepdims=True) + EPS) * kn_ref[...]

    q = _dot(cqb, wuq_ref[...])
    scale = B_QK_DIM ** -0.5 * LOG2E
    for h in range(B_HEADS):
        qh = q[:, h * B_QK_DIM:(h + 1) * B_QK_DIM].astype(BF16)
        qlat_ref[:, h * KV_RANK:(h + 1) * KV_RANK] = (_dot(qh, wukt_ref[h]) * scale).astype(BF16)
    qidxt_ref[...] = _dot(cqb, wiq_ref[...]).T.astype(BF16)

    rest = p[:, Q_RANK + KV_RANK:]
    widxt_ref[...] = rest.T[IDX_DIM:IDX_DIM + IDX_HEADS, :] * ((IDX_HEADS * IDX_DIM) ** -0.5)
    ckvx_ref[:, 0:KV_RANK] = ckv.astype(BF16)
    ckvx_ref[:, KV_RANK:] = jnp.ones((p.shape[0], KV_RANK), BF16)
    ckvt_ref[...] = ckv.T.astype(BF16)
    kidx_ref[...] = rest[:, 0:IDX_DIM].astype(BF16)


def _mla_prep(proj, q_norm, kv_norm, w_uq, w_ukt, w_iq, tm):
    s = proj.shape[0]
    return pl.pallas_call(
        _mla_prep_kernel,
        out_shape=[
            jax.ShapeDtypeStruct((s, B_HEADS * KV_RANK), BF16),
            jax.ShapeDtypeStruct((IDX_HEADS * IDX_DIM, s), BF16),
            jax.ShapeDtypeStruct((IDX_HEADS, s), F32),
            jax.ShapeDtypeStruct((s, 2 * KV_RANK), BF16),
            jax.ShapeDtypeStruct((KV_RANK, s), BF16),
            jax.ShapeDtypeStruct((s, IDX_DIM), BF16),
        ],
        grid=(s // tm,),
        in_specs=[
            pl.BlockSpec((tm, SMALL_W), lambda i: (i, COL_SMALL // SMALL_W)),
            _const_spec((1, Q_RANK)),
            _const_spec((1, KV_RANK)),
            _const_spec((Q_RANK, B_HEADS * B_QK_DIM)),
            _const_spec((B_HEADS, B_QK_DIM, KV_RANK)),
            _const_spec((Q_RANK, IDX_HEADS * IDX_DIM)),
        ],
        out_specs=[
            pl.BlockSpec((tm, B_HEADS * KV_RANK), lambda i: (i, 0)),
            pl.BlockSpec((IDX_HEADS * IDX_DIM, tm), lambda i: (0, i)),
            pl.BlockSpec((IDX_HEADS, tm), lambda i: (0, i)),
            pl.BlockSpec((tm, 2 * KV_RANK), lambda i: (i, 0)),
            pl.BlockSpec((KV_RANK, tm), lambda i: (0, i)),
            pl.BlockSpec((tm, IDX_DIM), lambda i: (i, 0)),
        ],
        compiler_params=_cparams(("arbitrary",)),
        name="mla_prep",
    )(proj, q_norm, kv_norm, w_uq, w_ukt, w_iq)


def _bias_tab_kernel(bkt_ref, rb_ref, o_ref):
    far = N_BUCKETS // 2 - 1
    for v in range(2):
        bkt = bkt_ref[v]
        for h in range(B_HEADS):
            acc = jnp.zeros(bkt.shape, F32)
            for k in range(N_BUCKETS):
                acc = jnp.where(bkt == k, (rb_ref[k, h] - rb_ref[far, h]) * LOG2E, acc)
            o_ref[v, h] = acc


def _bias_tab(buckets, rel_bias):
    return pl.pallas_call(
        _bias_tab_kernel,
        out_shape=jax.ShapeDtypeStruct((2, B_HEADS, TQ, TKN), F32),
        in_specs=[
            pl.BlockSpec(memory_space=pltpu.VMEM),
            pl.BlockSpec(memory_space=pltpu.SMEM),
        ],
        out_specs=pl.BlockSpec(memory_space=pltpu.VMEM),
        compiler_params=pltpu.CompilerParams(vmem_limit_bytes=VMEM_LIMIT),
        name="bias_tab",
    )(buckets, rel_bias)


def _t5_bucket(rel):
    half = N_BUCKETS // 2
    max_exact = half // 2
    ret = jnp.where(rel > 0, half, 0)
    n = jnp.abs(rel)
    nf = jnp.maximum(n, 1).astype(jnp.float32)
    large = max_exact + (jnp.log(nf / max_exact) / math.log(MAX_DISTANCE / max_exact) * (half - max_exact)).astype(jnp.int32)
    large = jnp.minimum(large, half - 1)
    return ret + jnp.where(n < max_exact, n, large)


def _dsa_kernel(qlat_ref, qidxt_ref, widxt_ref, kidx_ref, ckvt_ref, ckvx_ref, biasd_ref, wuv_ref,
                o_ref, isc_scr, t_scr, lhs_scr, rhs_scr, m_scr, acc_scr, p_scr, al_scr, qp_scr,
                *, top_k):
    qi = pl.program_id(0)
    q0 = qi * TQ
    kf = float(top_k)
    qlane = lax.broadcasted_iota(I32, (1, TQ), 1)
    lim = q0 + (qlane // CHUNK + 1) * CHUNK
    krow = lax.broadcasted_iota(I32, (TKA, TQ), 0)

    def fold(x, op):
        n = x.shape[0] // SUBLANES
        x = x.reshape(n, SUBLANES, TQ)
        while n > 1:
            n //= 2
            x = op(x[:n], x[n:2 * n])
        return x[0]

    def lanes_x(v, width):
        return jnp.concatenate([v] * (width // LANES), axis=1)

    na = (q0 + TQ + TKA - 1) // TKA
    qt = qidxt_ref[...]
    wt = widxt_ref[...]
    wp = []
    for p in range(IDX_HEADS // 2):
        qa = qt[(2 * p) * IDX_DIM:(2 * p + 1) * IDX_DIM, :]
        qb = qt[(2 * p + 1) * IDX_DIM:(2 * p + 2) * IDX_DIM, :]
        qp_scr[p] = jnp.concatenate([qa, qb], axis=1)
        wp.append(jnp.concatenate([wt[2 * p:2 * p + 1, :], wt[2 * p + 1:2 * p + 2, :]], axis=1))

    def scores(k0):
        kb = kidx_ref[pl.ds(k0, TKA), :]
        acc = jnp.zeros((TKA, TQ), F32)
        for p in range(IDX_HEADS // 2):
            s2 = jnp.maximum(_dot(kb, qp_scr[p]), 0.0) * wp[p]
            acc = acc + (s2[:, 0:TQ] + s2[:, TQ:])
        return acc

    def emit(k0, acc, stats, masked):
        mn, mx, c_ge0, c_gt0 = stats
        mn = jnp.minimum(mn, fold(acc, jnp.minimum))
        mx = jnp.maximum(mx, fold(acc, jnp.maximum))
        a = jnp.where((k0 + krow) < lim, acc, -jnp.inf) if masked else acc
        c_ge0 = c_ge0 + fold(jnp.where(a >= 0.0, 1.0, 0.0), jnp.add)
        c_gt0 = c_gt0 + fold(jnp.where(a > 0.0, 1.0, 0.0), jnp.add)
        isc_scr[pl.ds(k0, TKA), :] = a
        return mn, mx, c_ge0, c_gt0

    def phase_a(jb, stats, masked):
        k0 = pl.multiple_of(jb * (2 * TKA), 2 * TKA)
        a0 = scores(k0)
        a1 = scores(k0 + TKA)
        return emit(k0 + TKA, a1, emit(k0, a0, stats, masked), masked)

    zeros8 = jnp.zeros((SUBLANES, TQ), F32)
    n_trip = (na + 1) // 2
    stats = lax.fori_loop(
        0, n_trip - 1, functools.partial(phase_a, masked=False),
        (jnp.full((SUBLANES, TQ), jnp.inf, F32), jnp.full((SUBLANES, TQ), -jnp.inf, F32), zeros8, zeros8))
    mn, mx, pc_ge0, pc_gt0 = phase_a(n_trip - 1, stats, True)

    def count(pred):
        def body(g, cnt):
            k0 = pl.multiple_of(g * TKA, TKA)
            x = isc_scr[pl.ds(k0, TKA), :]
            return cnt + fold(jnp.where(pred(x, k0 + krow), 1.0, 0.0), jnp.add)
        return jnp.sum(lax.fori_loop(0, na, body, zeros8), axis=0, keepdims=True)

    t_scr[...] = jnp.full(t_scr.shape, -jnp.inf, F32)

    @pl.when(q0 + CHUNK > top_k)
    def _():
        def row_total(part):
            return jnp.sum(part, axis=0, keepdims=True)

        rmin = jnp.min(mn, axis=0, keepdims=True)
        rmax = jnp.max(mx, axis=0, keepdims=True)
        n_adm = lim.astype(F32)
        c_ge0, c_gt0 = row_total(pc_ge0), row_total(pc_gt0)
        above = c_gt0 >= kf
        below = c_ge0 < kf
        tiny = float(jnp.finfo(F32).tiny)
        hi_top = rmax + (rmax - rmin) + jnp.abs(rmax) * 1e-3 + 1e-30
        lo0 = jnp.where(below, rmin, 0.0)
        clo0 = jnp.where(below, n_adm, c_ge0)
        hi0 = jnp.where(below, 0.0, jnp.where(above, hi_top, tiny))
        chi0 = jnp.where(below, c_ge0, jnp.where(above, 0.0, c_gt0))
        log_k = math.log(kf)

        def plan(it, lo, hi, clo, chi, done):
            bis = 0.5 * lo + 0.5 * hi
            act = jnp.where(jnp.logical_and(done == 0.0, jnp.logical_and(bis > lo, bis < hi)), 1.0, 0.0)
            llo = jnp.log(clo)
            lhi = jnp.log(jnp.maximum(chi, 0.5))
            t = jnp.clip((llo - log_k) * pl.reciprocal(llo - lhi, approx=True), 0.03, 0.97)
            guess = lo + t * (hi - lo)
            use = jnp.logical_and(jnp.logical_and(guess > lo, guess < hi), it % 3 != 2)
            return act, jnp.sum(act), jnp.where(use, guess, bis)

        def count_ge(mid):
            def half(k0):
                return fold(jnp.where(isc_scr[pl.ds(k0, TKA), :] >= mid, 1.0, 0.0), jnp.add)

            def body(g, cnt):
                k0 = pl.multiple_of(g * (2 * TKA), 2 * TKA)
                return cnt + (half(k0) + half(k0 + TKA))

            return row_total(lax.fori_loop(0, n_trip, body, zeros8))

        def cond(st):
            it, nact = st[0], st[1]
            return jnp.logical_and(it < MAX_BISECT, nact > 0.0)

        def body(st):
            it, nact, lo, hi, clo, chi, done, act, mid = st
            c = count_ge(mid)
            on = act > 0.0
            up = jnp.logical_and(on, c >= kf)
            dn = jnp.logical_and(on, c < kf)
            lo = jnp.where(up, mid, lo)
            clo = jnp.where(up, c, clo)
            hi = jnp.where(dn, mid, hi)
            chi = jnp.where(dn, c, chi)
            done = jnp.where(jnp.logical_and(on, c == kf), 1.0, done)
            act, nact, mid = plan(it + 1, lo, hi, clo, chi, done)
            return it + 1, nact, lo, hi, clo, chi, done, act, mid

        done0 = jnp.where(c_ge0 == kf, 1.0, 0.0)
        act0, nact0, mid0 = plan(0, lo0, hi0, clo0, chi0, done0)
        st = lax.while_loop(cond, body, (jnp.int32(0), nact0, lo0, hi0, clo0, chi0, done0, act0, mid0))
        t_lo, t_hi, done = st[2], st[3], st[6]
        t_scr[0:1, :] = t_lo

        n_tie = jnp.sum(jnp.where(done == 0.0, 1.0, 0.0))

        @pl.when(n_tie > 0.0)
        def _():
            lo, hi, tie = t_lo, t_hi, done == 0.0
            r = kf - count(lambda x, idx: x >= hi)
            p = jnp.zeros((1, TQ), I32)
            nbits = max(1, (isc_scr.shape[0] - 1).bit_length())
            for bit in range(nbits - 1, -1, -1):
                cand = p + (1 << bit)
                f = count(lambda x, idx: jnp.logical_and(jnp.logical_and(x >= lo, x < hi), idx < cand))
                p = jnp.where(f < r, cand, p)

            def drop(g, carry):
                k0 = pl.multiple_of(g * TKA, TKA)
                x = isc_scr[pl.ds(k0, TKA), :]
                ex = jnp.logical_and(jnp.logical_and(tie, (k0 + krow) > p),
                                     jnp.logical_and(x >= lo, x < hi))
                isc_scr[pl.ds(k0, TKA), :] = jnp.where(ex, -jnp.inf, x)
                return carry

            lax.fori_loop(0, na, drop, 0)

    thr = t_scr[0:1, :]
    for h in range(B_HEADS):
        lhs_scr[h * TQ:(h + 1) * TQ, 0:KV_RANK] = qlat_ref[:, h * KV_RANK:(h + 1) * KV_RANK]

    @pl.when(qi == 0)
    def _():
        eye = jnp.where(lax.broadcasted_iota(I32, (TQ, TQ), 0) == lax.broadcasted_iota(I32, (TQ, TQ), 1),
                        1.0, 0.0).astype(BF16)
        for h in range(B_HEADS):
            lhs_scr[h * TQ:(h + 1) * TQ, KV_RANK:] = eye

    m_scr[...] = jnp.full(m_scr.shape, NEG, F32)
    acc_scr[...] = jnp.zeros_like(acc_scr)

    def accumulate(h, k0_prev, w_prev):
        rs = slice(h * TQ, (h + 1) * TQ)
        kv = ckvx_ref[pl.ds(k0_prev, w_prev), :]
        acc_scr[rs, :] = (acc_scr[rs, :] * lanes_x(al_scr[rs, :], 2 * KV_RANK)
                          + _dot(p_scr[rs, 0:w_prev], kv))

    def stage(k0_prev, w_prev, k0, width, key_lim, with_bias):
        x = isc_scr[pl.ds(k0, width), :]
        sel = jnp.logical_and(x >= thr, (k0 + krow[0:width, :]) < key_lim)
        rhs_scr[0:KV_RANK, 0:width] = ckvt_ref[:, pl.ds(k0, width)]
        rhs_scr[KV_RANK:, 0:width] = jnp.where(sel, 0.0, NEG).T.astype(BF16)
        rhs = rhs_scr[:, 0:width]
        for h in range(B_HEADS):
            rs = slice(h * TQ, (h + 1) * TQ)
            if w_prev:
                accumulate(h, k0_prev, w_prev)
            s = _dot(lhs_scr[rs, :], rhs)
            if with_bias:
                s = s + biasd_ref[0, h]
            m_old = m_scr[rs, :]
            m_new = jnp.maximum(m_old, jnp.max(s, axis=1, keepdims=True))
            al_scr[rs, :] = jnp.exp2(m_old - m_new)
            p_scr[rs, 0:width] = jnp.exp2(s - lanes_x(m_new, width)).astype(BF16)
            m_scr[rs, :] = m_new

    far_lim = jnp.maximum(q0 - TQ, 0)
    nf = (far_lim + TK - 1) // TK

    @pl.when(nf == 0)
    def _():
        p_scr[...] = jnp.zeros_like(p_scr)
        al_scr[...] = jnp.ones_like(al_scr)

    @pl.when(nf > 0)
    def _():
        stage(0, 0, 0, TK, far_lim, False)

    def far_stage(jb):
        stage(pl.multiple_of((jb - 1) * TK, TK), TK, pl.multiple_of(jb * TK, TK), TK, far_lim, False)

    def far_group(jj, carry):
        for u in range(FAR_UNROLL):
            far_stage(FAR_UNROLL * jj + 1 + u)
        return carry

    n_group = jnp.maximum(nf - 1, 0) // FAR_UNROLL
    lax.fori_loop(0, n_group, far_group, 0)

    def far_single(jb, carry):
        far_stage(jb)
        return carry

    lax.fori_loop(n_group * FAR_UNROLL + 1, nf, far_single, 0)

    k0_last = pl.multiple_of(jnp.maximum(nf - 1, 0) * TK, TK)
    k0_near = pl.multiple_of(far_lim, TQ)
    stage(k0_last, TK, k0_near, TKN, lim, True)
    for h in range(B_HEADS):
        accumulate(h, k0_near, TKN)

    for h in range(B_HEADS):
        rs = slice(h * TQ, (h + 1) * TQ)
        a = acc_scr[rs, :]
        olat = (a[:, 0:KV_RANK] / a[:, KV_RANK:]).astype(BF16)
        o_ref[:, h * B_V_DIM:(h + 1) * B_V_DIM] = _dot(olat, wuv_ref[h]).astype(o_ref.dtype)


def _dsa(qlat, qidxt, widxt, kidx, ckvt, ckvx, biasd, w_uv, top_k):
    s = qlat.shape[0]
    assert s % (2 * TKA) == 0 and TK == TKA and top_k % TQ == 0
    return pl.pallas_call(
        functools.partial(_dsa_kernel, top_k=top_k),
        out_shape=jax.ShapeDtypeStruct((s, B_WIDTH), BF16),
        grid=(s // TQ,),
        in_specs=[
            pl.BlockSpec((TQ, B_HEADS * KV_RANK), lambda i: (i, 0)),
            pl.BlockSpec((IDX_HEADS * IDX_DIM, TQ), lambda i: (0, i)),
            pl.BlockSpec((IDX_HEADS, TQ), lambda i: (0, i)),
            _const_spec((s, IDX_DIM)),
            _const_spec((KV_RANK, s)),
            _const_spec((s, 2 * KV_RANK)),
            pl.BlockSpec((1, B_HEADS, TQ, TKN), lambda i: (jnp.minimum(i, 1), 0, 0, 0)),
            _const_spec((B_HEADS, KV_RANK, B_V_DIM)),
        ],
        out_specs=pl.BlockSpec((TQ, B_WIDTH), lambda i: (i, 0)),
        scratch_shapes=[
            pltpu.VMEM((s, TQ), F32),
            pltpu.VMEM((SUBLANES, TQ), F32),
            pltpu.VMEM((B_HEADS * TQ, 2 * KV_RANK), BF16),
            pltpu.VMEM((2 * KV_RANK, TK), BF16),
            pltpu.VMEM((B_HEADS * TQ, LANES), F32),
            pltpu.VMEM((B_HEADS * TQ, 2 * KV_RANK), F32),
            pltpu.VMEM((B_HEADS * TQ, TK), BF16),
            pltpu.VMEM((B_HEADS * TQ, LANES), F32),
            pltpu.VMEM((IDX_HEADS // 2, IDX_DIM, 2 * TQ), BF16),
        ],
        compiler_params=_cparams(("arbitrary",)),
        name="dsa",
    )(qlat, qidxt, widxt, kidx, ckvt, ckvx, biasd, w_uv)


def _merge_kernel(x_ref, hg_ref, ob_ref, ga_ref, gb_ref, wa_ref, wb_ref, wo_ref, fn_ref,
                  x1_ref, h2_ref):
    ya = _dot(hg_ref[...], wa_ref[...])
    yb = _dot(ob_ref[...], wb_ref[...])
    mix = _sigmoid(ga_ref[...]) * ya + _sigmoid(gb_ref[...]) * yb
    x1 = x_ref[...] + _dot(mix.astype(BF16), wo_ref[...])
    x1_ref[...] = x1
    h2 = x1 * lax.rsqrt(jnp.mean(x1 * x1, axis=-1, keepdims=True) + EPS) * fn_ref[...]
    h2_ref[...] = h2.astype(BF16)


def _merge(x2, hg, ob, proj, w_a, w_b, w_o, ffn_norm, tm):
    s, d = x2.shape
    gblk = COL_GATE // d
    row = lambda i: (i, 0)
    return pl.pallas_call(
        _merge_kernel,
        out_shape=[jax.ShapeDtypeStruct((s, d), F32), jax.ShapeDtypeStruct((s, d), BF16)],
        grid=(s // tm,),
        in_specs=[
            pl.BlockSpec((tm, d), row),
            pl.BlockSpec((tm, A_WIDTH), row),
            pl.BlockSpec((tm, B_WIDTH), row),
            pl.BlockSpec((tm, d), lambda i: (i, gblk)),
            pl.BlockSpec((tm, d), lambda i: (i, gblk + 1)),
            _const_spec((A_WIDTH, d)),
            _const_spec((B_WIDTH, d)),
            _const_spec((d, d)),
            _const_spec((1, d)),
        ],
        out_specs=[pl.BlockSpec((tm, d), row), pl.BlockSpec((tm, d), row)],
        compiler_params=_cparams(("arbitrary",)),
        name="merge",
    )(x2, hg, ob, proj, proj, w_a, w_b, w_o, ffn_norm)


def _ffn_kernel(x1_ref, h2_ref, wup_ref, cw_ref, cb_ref, wdn_ref, fin_ref, o_ref,
                tail_scr, u_scr, act_scr, *, cbw):
    tm = h2_ref.shape[0]
    dff = wdn_ref.shape[0]
    halo = SUBLANES

    @pl.when(pl.program_id(0) == 0)
    def _():
        tail_scr[...] = jnp.zeros_like(tail_scr)

    h2 = h2_ref[...]

    def conv_cols(c0):
        u = _dot(h2, wup_ref[:, c0:c0 + cbw])
        u_scr[0:halo, :] = tail_scr[:, c0:c0 + cbw]
        u_scr[halo:, :] = u
        tail_scr[:, c0:c0 + cbw] = u[tm - halo:, :]
        cw = cw_ref[:, c0:c0 + cbw]
        return (cw[2:3] * u + cw[1:2] * u_scr[halo - 1:halo - 1 + tm, :]
                + cw[0:1] * u_scr[halo - 2:halo - 2 + tm, :] + cb_ref[:, c0:c0 + cbw])

    for j in range(dff // cbw):
        a = conv_cols(j * cbw)
        v = conv_cols(dff + j * cbw)
        act_scr[:, j * cbw:(j + 1) * cbw] = (a * _sigmoid(a) * v).astype(BF16)

    y = x1_ref[...] + _dot(act_scr[...], wdn_ref[...])
    o_ref[...] = y * lax.rsqrt(jnp.mean(y * y, axis=-1, keepdims=True) + EPS) * fin_ref[...]


def _ffn(x1, h2, w_up, conv_w, conv_b, w_down, final_norm, tm, cbw):
    s, d = x1.shape
    dff = w_down.shape[0]
    row = lambda i: (i, 0)
    return pl.pallas_call(
        functools.partial(_ffn_kernel, cbw=cbw),
        out_shape=jax.ShapeDtypeStruct((s, d), F32),
        grid=(s // tm,),
        in_specs=[
            pl.BlockSpec((tm, d), row),
            pl.BlockSpec((tm, d), row),
            _const_spec((d, 2 * dff)),
            _const_spec((CONV_WIDTH, 2 * dff)),
            _const_spec((1, 2 * dff)),
            _const_spec((dff, d)),
            _const_spec((1, d)),
        ],
        out_specs=pl.BlockSpec((tm, d), row),
        scratch_shapes=[
            pltpu.VMEM((SUBLANES, 2 * dff), F32),
            pltpu.VMEM((tm + SUBLANES, cbw), F32),
            pltpu.VMEM((tm, dff), BF16),
        ],
        compiler_params=_cparams(("arbitrary",)),
        name="ffn",
    )(x1, h2, w_up, conv_w, conv_b, w_down, final_norm)


def _row_tile(s, want):
    t = min(want, s)
    while s % t:
        t //= 2
    return t


def _tiles(s):
    return dict(
        inproj_rows=_row_tile(s, 1024),
        inproj_cols=N_PROJ // 4,
        hgrn_rows=_row_tile(s, 512),
        prep_rows=_row_tile(s, 512),
        merge_rows=_row_tile(s, 512),
        ffn_rows=_row_tile(s, 512),
        ffn_cols=256,
    )


def kernel(x, attn_norm, w_in, lower_bounds, hgrn_out_norm, q_norm, kv_norm, w_uq, w_uk, w_uv, w_iq,
           rel_bias, w_branch_a, w_branch_b, w_o, ffn_norm, w_up, conv_w, conv_b, w_down, final_norm):
    bsz, s, d = x.shape
    assert bsz == 1 and attn_norm.shape[0] == 1 and lower_bounds.shape[0] == 2
    x2 = x.reshape(s, d)
    top_k = min(TOPK_MAX, s // 4)

    wi = w_in[0]
    n_small = Q_RANK + KV_RANK + IDX_DIM + IDX_HEADS
    w_pad = jnp.concatenate(
        [wi[:, :COL_GATE].astype(BF16), wi[:, COL_GATE + n_small:].astype(BF16),
         wi[:, COL_GATE:COL_GATE + n_small].astype(BF16), jnp.zeros((d, SMALL_W - n_small), BF16)], axis=1)

    t = _tiles(s)
    proj = _inproj(x2, attn_norm, w_pad, t["inproj_rows"], t["inproj_cols"])

    hg = _hgrn(proj, lower_bounds, hgrn_out_norm, t["hgrn_rows"], A_HEADS)

    qlat, qidxt, widxt, ckvx, ckvt, kidx = _mla_prep(
        proj, q_norm, kv_norm, w_uq[0].astype(BF16), jnp.swapaxes(w_uk[0], 1, 2).astype(BF16),
        w_iq[0].astype(BF16), t["prep_rows"])

    t_loc = jnp.arange(TQ, dtype=jnp.int32)[None, :, None]
    s_loc = jnp.arange(TKN, dtype=jnp.int32)[None, None, :]
    off = jnp.array([0, -TQ], jnp.int32)[:, None, None]
    buckets = _t5_bucket(s_loc - t_loc + off)
    biasd = _bias_tab(buckets, rel_bias)

    ob = _dsa(qlat, qidxt, widxt, kidx, ckvt, ckvx, biasd, w_uv[0].astype(BF16), top_k)

    x1, h2 = _merge(x2, hg, ob, proj, w_branch_a[0].astype(BF16), w_branch_b[0].astype(BF16),
                    w_o[0].astype(BF16), ffn_norm, t["merge_rows"])

    out = _ffn(x1, h2, w_up[0].astype(BF16), conv_w[0], conv_b, w_down[0].astype(BF16),
               final_norm.reshape(1, d), t["ffn_rows"], t["ffn_cols"])
    return out.reshape(bsz, s, d)
```

```python
import functools
import math

import jax
import jax.numpy as jnp
from jax import lax
from jax.experimental import pallas as pl
from jax.experimental.pallas import tpu as pltpu

F32 = jnp.float32
BF16 = jnp.bfloat16
I32 = jnp.int32

EPS = 1e-6
CHUNK = 64
A_HEADS = 8
A_HEAD_DIM = 128
A_WIDTH = A_HEADS * A_HEAD_DIM
B_HEADS = 16
B_QK_DIM = 64
B_V_DIM = 64
B_WIDTH = B_HEADS * B_V_DIM
Q_RANK = 256
KV_RANK = 128
IDX_HEADS = 8
IDX_DIM = 64
TOPK_MAX = 256
N_BUCKETS = 32
MAX_DISTANCE = 128
CONV_WIDTH = 3

LANES = 128
SUBLANES = 8
VMEM_LIMIT = 56 * 1024 * 1024

COL_QFIG = 0
COL_GATE = 4 * A_WIDTH
COL_SMALL = COL_GATE + 2048
SMALL_W = 512
N_PROJ = COL_SMALL + SMALL_W

TQ = 128
TKA = 512
TK = 512
TKN = 2 * TQ
FAR_UNROLL = 2
NEG = -1e30
EXP_SAFE = 80.0
MAX_BISECT = 1024
LOG2E = 1.4426950408889634


def _cparams(sem):
    return pltpu.CompilerParams(dimension_semantics=sem, vmem_limit_bytes=VMEM_LIMIT)


def _const_spec(shape):
    nd = len(shape)
    return pl.BlockSpec(shape, lambda *_: (0,) * nd, pipeline_mode=pl.Buffered(1))


def _dot(a, b):
    return jnp.dot(a, b, preferred_element_type=F32)


def _dot_nt(a, b):
    return lax.dot_general(a, b, (((1,), (1,)), ((), ())), preferred_element_type=F32)


def _dot_tn(a, b):
    return lax.dot_general(a, b, (((0,), (0,)), ((), ())), preferred_element_type=F32)


def _sigmoid(x):
    return jax.nn.sigmoid(x)


def _inproj_kernel(x_ref, g_ref, w_ref, o_ref, h_scr):
    @pl.when(pl.program_id(1) == 0)
    def _():
        x = x_ref[...]
        y = x * lax.rsqrt(jnp.mean(x * x, axis=-1, keepdims=True) + EPS) * g_ref[...]
        h_scr[...] = y.astype(BF16)

    o_ref[...] = _dot(h_scr[...], w_ref[...])


def _inproj(x2, gain, w_pad, tm, tn):
    s, d = x2.shape
    n = w_pad.shape[1]
    return pl.pallas_call(
        _inproj_kernel,
        out_shape=jax.ShapeDtypeStruct((s, n), F32),
        grid=(s // tm, n // tn),
        in_specs=[
            pl.BlockSpec((tm, d), lambda i, j: (i, 0)),
            pl.BlockSpec((1, d), lambda i, j: (0, 0)),
            pl.BlockSpec((d, tn), lambda i, j: (0, j)),
        ],
        out_specs=pl.BlockSpec((tm, tn), lambda i, j: (i, j)),
        scratch_shapes=[pltpu.VMEM((tm, d), BF16)],
        compiler_params=_cparams(("arbitrary", "arbitrary")),
        name="inproj",
    )(x2, gain, w_pad)


def _hgrn_kernel(q_ref, f_ref, i_ref, g_ref, lbp_ref, on_ref, o_ref,
                 st_ref, b_scr, k_scr, qs_scr, att_scr, *, nch):
    c_ = CHUNK
    half = c_ // 2

    @pl.when(pl.program_id(1) == 0)
    def _():
        st_ref[...] = jnp.zeros_like(st_ref)

    l0 = lbp_ref[0:1, :]
    l1 = lbp_ref[1:2, :]
    lm = jnp.maximum(l0, l1)
    e0 = jnp.exp(l0 - lm)
    e1 = jnp.exp(l1 - lm)
    lb = e0 / (e0 + e1)

    row = lax.broadcasted_iota(I32, (c_, c_), 0)
    col = lax.broadcasted_iota(I32, (c_, c_), 1)
    tri = row >= col
    tri_bf = jnp.where(tri, 1.0, 0.0).astype(BF16)

    q = q_ref[...]
    qs_scr[...] = q * _sigmoid(q)
    fg = lb + (1.0 - lb) * _sigmoid(f_ref[...])
    k_scr[...] = 1.0 - fg
    lg = jnp.log(fg)

    d = A_HEAD_DIM
    hpg = q_ref.shape[1] // d
    heads = [slice(hh * d, (hh + 1) * d) for hh in range(hpg)]

    dev = jnp.zeros((1, 1), F32)
    for c in range(nch):
        l = lg[c * c_:(c + 1) * c_]
        hi = l.astype(BF16)
        r1 = l - hi.astype(F32)
        mid = r1.astype(BF16)
        lo = (r1 - mid.astype(F32)).astype(BF16)
        b = _dot(tri_bf, hi) + _dot(tri_bf, mid) + _dot(tri_bf, lo)
        b_scr[c * c_:(c + 1) * c_, :] = b
        rho = b[half - 1:half, :]
        dev = jnp.maximum(dev, jnp.max(jnp.abs(b - rho), keepdims=True))
    safe = dev[0, 0] <= EXP_SAFE

    @pl.when(safe)
    def _():
        for c in range(nch):
            sl = slice(c * c_, (c + 1) * c_)
            b = b_scr[sl, :]
            rho = b[half - 1:half, :]
            qt = (qs_scr[sl, :] * jnp.exp(b - rho)).astype(BF16)
            kt = (k_scr[sl, :] * jnp.exp(rho - b)).astype(BF16)
            for hh, hs in enumerate(heads):
                att_scr[hh, c] = jnp.where(tri, _dot_nt(qt[:, hs], kt[:, hs]), 0.0)

    @pl.when(jnp.logical_not(safe))
    def _():
        trow = lax.broadcasted_iota(I32, (c_, d), 0)
        for c in range(nch):
            sl = slice(c * c_, (c + 1) * c_)
            for hh, hs in enumerate(heads):
                b = b_scr[sl, hs]
                qv = qs_scr[sl, hs]

                def body(s, att):
                    bs = b_scr[pl.ds(c * c_ + s, 1), :][:, hs]
                    ks = k_scr[pl.ds(c * c_ + s, 1), :][:, hs]
                    dd = jnp.where(trow >= s, b - bs, -jnp.inf)
                    colv = jnp.sum(qv * ks * jnp.exp(dd), axis=-1, keepdims=True)
                    return jnp.where(col == s, colv, att)

                att_scr[hh, c] = lax.fori_loop(0, c_, body, jnp.zeros((c_, c_), F32))

    on = on_ref[...]
    for c in range(nch):
        sl = slice(c * c_, (c + 1) * c_)
        b = b_scr[sl, :]
        bend = b[c_ - 1:c_, :]
        v = i_ref[sl, :].astype(BF16)
        qe = (qs_scr[sl, :] * jnp.exp(b)).astype(BF16)
        kd = (k_scr[sl, :] * jnp.exp(bend - b)).astype(BF16)
        decay = jnp.exp(bend)
        g = g_ref[sl, :]
        gate = g * _sigmoid(g)
        for hh, hs in enumerate(heads):
            st = st_ref[hh]
            o = _dot_nt(qe[:, hs], st.astype(BF16)) + _dot(att_scr[hh, c].astype(BF16), v[:, hs])
            st_ref[hh] = st * decay[:, hs] + _dot_tn(v[:, hs], kd[:, hs])
            o = o * lax.rsqrt(jnp.mean(o * o, axis=-1, keepdims=True) + EPS) * on
            o_ref[sl, hs] = (o * gate[:, hs]).astype(o_ref.dtype)


def _hgrn(proj, lower_bounds, out_norm, t_rows, hpg):
    s = proj.shape[0]
    nch = t_rows // CHUNK
    d = A_HEAD_DIM
    w = hpg * d
    nb = COL_QFIG // w

    def colspec(off):
        return pl.BlockSpec((t_rows, w), lambda h, i, off=off: (i, nb + off * (A_HEADS // hpg) + h))

    return pl.pallas_call(
        functools.partial(_hgrn_kernel, nch=nch),
        out_shape=jax.ShapeDtypeStruct((s, A_WIDTH), BF16),
        grid=(A_HEADS // hpg, s // t_rows),
        in_specs=[
            colspec(0), colspec(1), colspec(2), colspec(3),
            pl.BlockSpec((2, w), lambda h, i: (0, h)),
            pl.BlockSpec((1, d), lambda h, i: (0, 0)),
        ],
        out_specs=pl.BlockSpec((t_rows, w), lambda h, i: (i, h)),
        scratch_shapes=[
            pltpu.VMEM((hpg, d, d), F32),
            pltpu.VMEM((t_rows, w), F32),
            pltpu.VMEM((t_rows, w), F32),
            pltpu.VMEM((t_rows, w), F32),
            pltpu.VMEM((hpg, nch, CHUNK, CHUNK), F32),
        ],
        compiler_params=_cparams(("arbitrary", "arbitrary")),
        name="hgrn",
    )(proj, proj, proj, proj, lower_bounds, out_norm)


def _mla_prep_kernel(p_ref, qn_ref, kn_ref, wuq_ref, wukt_ref, wiq_ref,
                     qlat_ref, qidxt_ref, widxt_ref, ckvx_ref, ckvt_ref, kidx_ref):
    p = p_ref[...]
    cq = p[:, 0:Q_RANK]
    cq = cq * lax.rsqrt(jnp.mean(cq * cq, axis=-1, keepdims=True) + EPS) * qn_ref[...]
    cqb = cq.astype(BF16)
    ckv = p[:, Q_RANK:Q_RANK + KV_RANK]
    ckv = ckv * lax.rsqrt(jnp.mean(ckv * ckv, axis=-1, keepdims=True) + EPS) * kn_ref[...]

    q = _dot(cqb, wuq_ref[...])
    scale = B_QK_DIM ** -0.5 * LOG2E
    for h in range(B_HEADS):
        qh = q[:, h * B_QK_DIM:(h + 1) * B_QK_DIM].astype(BF16)
        qlat_ref[:, h * KV_RANK:(h + 1) * KV_RANK] = (_dot(qh, wukt_ref[h]) * scale).astype(BF16)
    qidxt_ref[...] = _dot(cqb, wiq_ref[...]).T.astype(BF16)

    rest = p[:, Q_RANK + KV_RANK:]
    widxt_ref[...] = rest.T[IDX_DIM:IDX_DIM + IDX_HEADS, :] * ((IDX_HEADS * IDX_DIM) ** -0.5)
    ckvx_ref[:, 0:KV_RANK] = ckv.astype(BF16)
    ckvx_ref[:, KV_RANK:] = jnp.ones((p.shape[0], KV_RANK), BF16)
    ckvt_ref[...] = ckv.T.astype(BF16)
    kidx_ref[...] = rest[:, 0:IDX_DIM].astype(BF16)


def _mla_prep(proj, q_norm, kv_norm, w_uq, w_ukt, w_iq, tm):
    s = proj.shape[0]
    return pl.pallas_call(
        _mla_prep_kernel,
        out_shape=[
            jax.ShapeDtypeStruct((s, B_HEADS * KV_RANK), BF16),
            jax.ShapeDtypeStruct((IDX_HEADS * IDX_DIM, s), BF16),
            jax.ShapeDtypeStruct((IDX_HEADS, s), F32),
            jax.ShapeDtypeStruct((s, 2 * KV_RANK), BF16),
            jax.ShapeDtypeStruct((KV_RANK, s), BF16),
            jax.ShapeDtypeStruct((s, IDX_DIM), BF16),
        ],
        grid=(s // tm,),
        in_specs=[
            pl.BlockSpec((tm, SMALL_W), lambda i: (i, COL_SMALL // SMALL_W)),
            _const_spec((1, Q_RANK)),
            _const_spec((1, KV_RANK)),
            _const_spec((Q_RANK, B_HEADS * B_QK_DIM)),
            _const_spec((B_HEADS, B_QK_DIM, KV_RANK)),
            _const_spec((Q_RANK, IDX_HEADS * IDX_DIM)),
        ],
        out_specs=[
            pl.BlockSpec((tm, B_HEADS * KV_RANK), lambda i: (i, 0)),
            pl.BlockSpec((IDX_HEADS * IDX_DIM, tm), lambda i: (0, i)),
            pl.BlockSpec((IDX_HEADS, tm), lambda i: (0, i)),
            pl.BlockSpec((tm, 2 * KV_RANK), lambda i: (i, 0)),
            pl.BlockSpec((KV_RANK, tm), lambda i: (0, i)),
            pl.BlockSpec((tm, IDX_DIM), lambda i: (i, 0)),
        ],
        compiler_params=_cparams(("arbitrary",)),
        name="mla_prep",
    )(proj, q_norm, kv_norm, w_uq, w_ukt, w_iq)


def _bias_tab_kernel(bkt_ref, rb_ref, o_ref):
    far = N_BUCKETS // 2 - 1
    for v in range(2):
        bkt = bkt_ref[v]
        for h in range(B_HEADS):
            acc = jnp.zeros(bkt.shape, F32)
            for k in range(N_BUCKETS):
                acc = jnp.where(bkt == k, (rb_ref[k, h] - rb_ref[far, h]) * LOG2E, acc)
            o_ref[v, h] = acc


def _bias_tab(buckets, rel_bias):
    return pl.pallas_call(
        _bias_tab_kernel,
        out_shape=jax.ShapeDtypeStruct((2, B_HEADS, TQ, TKN), F32),
        in_specs=[
            pl.BlockSpec(memory_space=pltpu.VMEM),
            pl.BlockSpec(memory_space=pltpu.SMEM),
        ],
        out_specs=pl.BlockSpec(memory_space=pltpu.VMEM),
        compiler_params=pltpu.CompilerParams(vmem_limit_bytes=VMEM_LIMIT),
        name="bias_tab",
    )(buckets, rel_bias)


def _t5_bucket(rel):
    half = N_BUCKETS // 2
    max_exact = half // 2
    ret = jnp.where(rel > 0, half, 0)
    n = jnp.abs(rel)
    nf = jnp.maximum(n, 1).astype(jnp.float32)
    large = max_exact + (jnp.log(nf / max_exact) / math.log(MAX_DISTANCE / max_exact) * (half - max_exact)).astype(jnp.int32)
    large = jnp.minimum(large, half - 1)
    return ret + jnp.where(n < max_exact, n, large)


def _dsa_kernel(qlat_ref, qidxt_ref, widxt_ref, kidx_ref, ckvt_ref, ckvx_ref, biasd_ref, wuv_ref,
                o_ref, isc_scr, t_scr, lhs_scr, rhs_scr, m_scr, acc_scr, p_scr, al_scr, qp_scr,
                *, top_k):
    qi = pl.program_id(0)
    q0 = qi * TQ
    kf = float(top_k)
    qlane = lax.broadcasted_iota(I32, (1, TQ), 1)
    lim = q0 + (qlane // CHUNK + 1) * CHUNK
    krow = lax.broadcasted_iota(I32, (TKA, TQ), 0)

    def fold(x, op):
        n = x.shape[0] // SUBLANES
        x = x.reshape(n, SUBLANES, TQ)
        while n > 1:
            n //= 2
            x = op(x[:n], x[n:2 * n])
        return x[0]

    def lanes_x(v, width):
        return jnp.concatenate([v] * (width // LANES), axis=1)

    na = (q0 + TQ + TKA - 1) // TKA
    qt = qidxt_ref[...]
    wt = widxt_ref[...]
    wp = []
    for p in range(IDX_HEADS // 2):
        qa = qt[(2 * p) * IDX_DIM:(2 * p + 1) * IDX_DIM, :]
        qb = qt[(2 * p + 1) * IDX_DIM:(2 * p + 2) * IDX_DIM, :]
        qp_scr[p] = jnp.concatenate([qa, qb], axis=1)
        wp.append(jnp.concatenate([wt[2 * p:2 * p + 1, :], wt[2 * p + 1:2 * p + 2, :]], axis=1))

    def scores(k0):
        kb = kidx_ref[pl.ds(k0, TKA), :]
        acc = jnp.zeros((TKA, TQ), F32)
        for p in range(IDX_HEADS // 2):
            s2 = jnp.maximum(_dot(kb, qp_scr[p]), 0.0) * wp[p]
            acc = acc + (s2[:, 0:TQ] + s2[:, TQ:])
        return acc

    def emit(k0, acc, stats, masked):
        mn, mx, c_ge0, c_gt0 = stats
        mn = jnp.minimum(mn, fold(acc, jnp.minimum))
        mx = jnp.maximum(mx, fold(acc, jnp.maximum))
        a = jnp.where((k0 + krow) < lim, acc, -jnp.inf) if masked else acc
        c_ge0 = c_ge0 + fold(jnp.where(a >= 0.0, 1.0, 0.0), jnp.add)
        c_gt0 = c_gt0 + fold(jnp.where(a > 0.0, 1.0, 0.0), jnp.add)
        isc_scr[pl.ds(k0, TKA), :] = a
        return mn, mx, c_ge0, c_gt0

    def phase_a(jb, stats, masked):
        k0 = pl.multiple_of(jb * (2 * TKA), 2 * TKA)
        a0 = scores(k0)
        a1 = scores(k0 + TKA)
        return emit(k0 + TKA, a1, emit(k0, a0, stats, masked), masked)

    zeros8 = jnp.zeros((SUBLANES, TQ), F32)
    n_trip = (na + 1) // 2
    stats = lax.fori_loop(
        0, n_trip - 1, functools.partial(phase_a, masked=False),
        (jnp.full((SUBLANES, TQ), jnp.inf, F32), jnp.full((SUBLANES, TQ), -jnp.inf, F32), zeros8, zeros8))
    mn, mx, pc_ge0, pc_gt0 = phase_a(n_trip - 1, stats, True)

    def count(pred):
        def body(g, cnt):
            k0 = pl.multiple_of(g * TKA, TKA)
            x = isc_scr[pl.ds(k0, TKA), :]
            return cnt + fold(jnp.where(pred(x, k0 + krow), 1.0, 0.0), jnp.add)
        return jnp.sum(lax.fori_loop(0, na, body, zeros8), axis=0, keepdims=True)

    t_scr[...] = jnp.full(t_scr.shape, -jnp.inf, F32)

    @pl.when(q0 + CHUNK > top_k)
    def _():
        def row_total(part):
            return jnp.sum(part, axis=0, keepdims=True)

        rmin = jnp.min(mn, axis=0, keepdims=True)
        rmax = jnp.max(mx, axis=0, keepdims=True)
        n_adm = lim.astype(F32)
        c_ge0, c_gt0 = row_total(pc_ge0), row_total(pc_gt0)
        above = c_gt0 >= kf
        below = c_ge0 < kf
        tiny = float(jnp.finfo(F32).tiny)
        hi_top = rmax + (rmax - rmin) + jnp.abs(rmax) * 1e-3 + 1e-30
        lo0 = jnp.where(below, rmin, 0.0)
        clo0 = jnp.where(below, n_adm, c_ge0)
        hi0 = jnp.where(below, 0.0, jnp.where(above, hi_top, tiny))
        chi0 = jnp.where(below, c_ge0, jnp.where(above, 0.0, c_gt0))
        log_k = math.log(kf)

        def plan(it, lo, hi, clo, chi, done):
            bis = 0.5 * lo + 0.5 * hi
            act = jnp.where(jnp.logical_and(done == 0.0, jnp.logical_and(bis > lo, bis < hi)), 1.0, 0.0)
            llo = jnp.log(clo)
            lhi = jnp.log(jnp.maximum(chi, 0.5))
            t = jnp.clip((llo - log_k) * pl.reciprocal(llo - lhi, approx=True), 0.03, 0.97)
            guess = lo + t * (hi - lo)
            use = jnp.logical_and(jnp.logical_and(guess > lo, guess < hi), it % 3 != 2)
            return act, jnp.sum(act), jnp.where(use, guess, bis)

        def count_ge(mid):
            def half(k0):
                return fold(jnp.where(isc_scr[pl.ds(k0, TKA), :] >= mid, 1.0, 0.0), jnp.add)

            def body(g, cnt):
                k0 = pl.multiple_of(g * (2 * TKA), 2 * TKA)
                return cnt + (half(k0) + half(k0 + TKA))

            return row_total(lax.fori_loop(0, n_trip, body, zeros8))

        def cond(st):
            it, nact = st[0], st[1]
            return jnp.logical_and(it < MAX_BISECT, nact > 0.0)

        def body(st):
            it, nact, lo, hi, clo, chi, done, act, mid = st
            c = count_ge(mid)
            on = act > 0.0
            up = jnp.logical_and(on, c >= kf)
            dn = jnp.logical_and(on, c < kf)
            lo = jnp.where(up, mid, lo)
            clo = jnp.where(up, c, clo)
            hi = jnp.where(dn, mid, hi)
            chi = jnp.where(dn, c, chi)
            done = jnp.where(jnp.logical_and(on, c == kf), 1.0, done)
            act, nact, mid = plan(it + 1, lo, hi, clo, chi, done)
            return it + 1, nact, lo, hi, clo, chi, done, act, mid

        done0 = jnp.where(c_ge0 == kf, 1.0, 0.0)
        act0, nact0, mid0 = plan(0, lo0, hi0, clo0, chi0, done0)
        st = lax.while_loop(cond, body, (jnp.int32(0), nact0, lo0, hi0, clo0, chi0, done0, act0, mid0))
        t_lo, t_hi, done = st[2], st[3], st[6]
        t_scr[0:1, :] = t_lo

        n_tie = jnp.sum(jnp.where(done == 0.0, 1.0, 0.0))

        @pl.when(n_tie > 0.0)
        def _():
            lo, hi, tie = t_lo, t_hi, done == 0.0
            r = kf - count(lambda x, idx: x >= hi)
            p = jnp.zeros((1, TQ), I32)
            nbits = max(1, (isc_scr.shape[0] - 1).bit_length())
            for bit in range(nbits - 1, -1, -1):
                cand = p + (1 << bit)
                f = count(lambda x, idx: jnp.logical_and(jnp.logical_and(x >= lo, x < hi), idx < cand))
                p = jnp.where(f < r, cand, p)

            def drop(g, carry):
                k0 = pl.multiple_of(g * TKA, TKA)
                x = isc_scr[pl.ds(k0, TKA), :]
                ex = jnp.logical_and(jnp.logical_and(tie, (k0 + krow) > p),
                                     jnp.logical_and(x >= lo, x < hi))
                isc_scr[pl.ds(k0, TKA), :] = jnp.where(ex, -jnp.inf, x)
                return carry

            lax.fori_loop(0, na, drop, 0)

    thr = t_scr[0:1, :]
    for h in range(B_HEADS):
        lhs_scr[h * TQ:(h + 1) * TQ, 0:KV_RANK] = qlat_ref[:, h * KV_RANK:(h + 1) * KV_RANK]

    @pl.when(qi == 0)
    def _():
        eye = jnp.where(lax.broadcasted_iota(I32, (TQ, TQ), 0) == lax.broadcasted_iota(I32, (TQ, TQ), 1),
                        1.0, 0.0).astype(BF16)
        for h in range(B_HEADS):
            lhs_scr[h * TQ:(h + 1) * TQ, KV_RANK:] = eye

    m_scr[...] = jnp.full(m_scr.shape, NEG, F32)
    acc_scr[...] = jnp.zeros_like(acc_scr)

    def accumulate(h, k0_prev, w_prev):
        rs = slice(h * TQ, (h + 1) * TQ)
        kv = ckvx_ref[pl.ds(k0_prev, w_prev), :]
        acc_scr[rs, :] = (acc_scr[rs, :] * lanes_x(al_scr[rs, :], 2 * KV_RANK)
                          + _dot(p_scr[rs, 0:w_prev], kv))

    def stage(k0_prev, w_prev, k0, width, key_lim, with_bias):
        x = isc_scr[pl.ds(k0, width), :]
        sel = jnp.logical_and(x >= thr, (k0 + krow[0:width, :]) < key_lim)
        rhs_scr[0:KV_RANK, 0:width] = ckvt_ref[:, pl.ds(k0, width)]
        rhs_scr[KV_RANK:, 0:width] = jnp.where(sel, 0.0, NEG).T.astype(BF16)
        rhs = rhs_scr[:, 0:width]
        for h in range(B_HEADS):
            rs = slice(h * TQ, (h + 1) * TQ)
            if w_prev:
                accumulate(h, k0_prev, w_prev)
            s = _dot(lhs_scr[rs, :], rhs)
            if with_bias:
                s = s + biasd_ref[0, h]
            m_old = m_scr[rs, :]
            m_new = jnp.maximum(m_old, jnp.max(s, axis=1, keepdims=True))
            al_scr[rs, :] = jnp.exp2(m_old - m_new)
            p_scr[rs, 0:width] = jnp.exp2(s - lanes_x(m_new, width)).astype(BF16)
            m_scr[rs, :] = m_new

    far_lim = jnp.maximum(q0 - TQ, 0)
    nf = (far_lim + TK - 1) // TK

    @pl.when(nf == 0)
    def _():
        p_scr[...] = jnp.zeros_like(p_scr)
        al_scr[...] = jnp.ones_like(al_scr)

    @pl.when(nf > 0)
    def _():
        stage(0, 0, 0, TK, far_lim, False)

    def far_stage(jb):
        stage(pl.multiple_of((jb - 1) * TK, TK), TK, pl.multiple_of(jb * TK, TK), TK, far_lim, False)

    def far_group(jj, carry):
        for u in range(FAR_UNROLL):
            far_stage(FAR_UNROLL * jj + 1 + u)
        return carry

    n_group = jnp.maximum(nf - 1, 0) // FAR_UNROLL
    lax.fori_loop(0, n_group, far_group, 0)

    def far_single(jb, carry):
        far_stage(jb)
        return carry

    lax.fori_loop(n_group * FAR_UNROLL + 1, nf, far_single, 0)

    k0_last = pl.multiple_of(jnp.maximum(nf - 1, 0) * TK, TK)
    k0_near = pl.multiple_of(far_lim, TQ)
    stage(k0_last, TK, k0_near, TKN, lim, True)
    for h in range(B_HEADS):
        accumulate(h, k0_near, TKN)

    for h in range(B_HEADS):
        rs = slice(h * TQ, (h + 1) * TQ)
        a = acc_scr[rs, :]
        olat = (a[:, 0:KV_RANK] / a[:, KV_RANK:]).astype(BF16)
        o_ref[:, h * B_V_DIM:(h + 1) * B_V_DIM] = _dot(olat, wuv_ref[h]).astype(o_ref.dtype)


def _dsa(qlat, qidxt, widxt, kidx, ckvt, ckvx, biasd, w_uv, top_k):
    s = qlat.shape[0]
    assert s % (2 * TKA) == 0 and TK == TKA and top_k % TQ == 0
    return pl.pallas_call(
        functools.partial(_dsa_kernel, top_k=top_k),
        out_shape=jax.ShapeDtypeStruct((s, B_WIDTH), BF16),
        grid=(s // TQ,),
        in_specs=[
            pl.BlockSpec((TQ, B_HEADS * KV_RANK), lambda i: (i, 0)),
            pl.BlockSpec((IDX_HEADS * IDX_DIM, TQ), lambda i: (0, i)),
            pl.BlockSpec((IDX_HEADS, TQ), lambda i: (0, i)),
            _const_spec((s, IDX_DIM)),
            _const_spec((KV_RANK, s)),
            _const_spec((s, 2 * KV_RANK)),
            pl.BlockSpec((1, B_HEADS, TQ, TKN), lambda i: (jnp.minimum(i, 1), 0, 0, 0)),
            _const_spec((B_HEADS, KV_RANK, B_V_DIM)),
        ],
        out_specs=pl.BlockSpec((TQ, B_WIDTH), lambda i: (i, 0)),
        scratch_shapes=[
            pltpu.VMEM((s, TQ), F32),
            pltpu.VMEM((SUBLANES, TQ), F32),
            pltpu.VMEM((B_HEADS * TQ, 2 * KV_RANK), BF16),
            pltpu.VMEM((2 * KV_RANK, TK), BF16),
            pltpu.VMEM((B_HEADS * TQ, LANES), F32),
            pltpu.VMEM((B_HEADS * TQ, 2 * KV_RANK), F32),
            pltpu.VMEM((B_HEADS * TQ, TK), BF16),
            pltpu.VMEM((B_HEADS * TQ, LANES), F32),
            pltpu.VMEM((IDX_HEADS // 2, IDX_DIM, 2 * TQ), BF16),
        ],
        compiler_params=_cparams(("arbitrary",)),
        name="dsa",
    )(qlat, qidxt, widxt, kidx, ckvt, ckvx, biasd, w_uv)


def _merge_kernel(x_ref, hg_ref, ob_ref, ga_ref, gb_ref, wa_ref, wb_ref, wo_ref, fn_ref,
                  x1_ref, h2_ref):
    ya = _dot(hg_ref[...], wa_ref[...])
    yb = _dot(ob_ref[...], wb_ref[...])
    mix = _sigmoid(ga_ref[...]) * ya + _sigmoid(gb_ref[...]) * yb
    x1 = x_ref[...] + _dot(mix.astype(BF16), wo_ref[...])
    x1_ref[...] = x1
    h2 = x1 * lax.rsqrt(jnp.mean(x1 * x1, axis=-1, keepdims=True) + EPS) * fn_ref[...]
    h2_ref[...] = h2.astype(BF16)


def _merge(x2, hg, ob, proj, w_a, w_b, w_o, ffn_norm, tm):
    s, d = x2.shape
    gblk = COL_GATE // d
    row = lambda i: (i, 0)
    return pl.pallas_call(
        _merge_kernel,
        out_shape=[jax.ShapeDtypeStruct((s, d), F32), jax.ShapeDtypeStruct((s, d), BF16)],
        grid=(s // tm,),
        in_specs=[
            pl.BlockSpec((tm, d), row),
            pl.BlockSpec((tm, A_WIDTH), row),
            pl.BlockSpec((tm, B_WIDTH), row),
            pl.BlockSpec((tm, d), lambda i: (i, gblk)),
            pl.BlockSpec((tm, d), lambda i: (i, gblk + 1)),
            _const_spec((A_WIDTH, d)),
            _const_spec((B_WIDTH, d)),
            _const_spec((d, d)),
            _const_spec((1, d)),
        ],
        out_specs=[pl.BlockSpec((tm, d), row), pl.BlockSpec((tm, d), row)],
        compiler_params=_cparams(("arbitrary",)),
        name="merge",
    )(x2, hg, ob, proj, proj, w_a, w_b, w_o, ffn_norm)


def _ffn_kernel(x1_ref, h2_ref, wup_ref, cw_ref, cb_ref, wdn_ref, fin_ref, o_ref,
                tail_scr, u_scr, act_scr, *, cbw):
    tm = h2_ref.shape[0]
    dff = wdn_ref.shape[0]
    halo = SUBLANES

    @pl.when(pl.program_id(0) == 0)
    def _():
        tail_scr[...] = jnp.zeros_like(tail_scr)

    h2 = h2_ref[...]

    def conv_cols(c0):
        u = _dot(h2, wup_ref[:, c0:c0 + cbw])
        u_scr[0:halo, :] = tail_scr[:, c0:c0 + cbw]
        u_scr[halo:, :] = u
        tail_scr[:, c0:c0 + cbw] = u[tm - halo:, :]
        cw = cw_ref[:, c0:c0 + cbw]
        return (cw[2:3] * u + cw[1:2] * u_scr[halo - 1:halo - 1 + tm, :]
                + cw[0:1] * u_scr[halo - 2:halo - 2 + tm, :] + cb_ref[:, c0:c0 + cbw])

    for j in range(dff // cbw):
        a = conv_cols(j * cbw)
        v = conv_cols(dff + j * cbw)
        act_scr[:, j * cbw:(j + 1) * cbw] = (a * _sigmoid(a) * v).astype(BF16)

    y = x1_ref[...] + _dot(act_scr[...], wdn_ref[...])
    o_ref[...] = y * lax.rsqrt(jnp.mean(y * y, axis=-1, keepdims=True) + EPS) * fin_ref[...]


def _ffn(x1, h2, w_up, conv_w, conv_b, w_down, final_norm, tm, cbw):
    s, d = x1.shape
    dff = w_down.shape[0]
    row = lambda i: (i, 0)
    return pl.pallas_call(
        functools.partial(_ffn_kernel, cbw=cbw),
        out_shape=jax.ShapeDtypeStruct((s, d), F32),
        grid=(s // tm,),
        in_specs=[
            pl.BlockSpec((tm, d), row),
            pl.BlockSpec((tm, d), row),
            _const_spec((d, 2 * dff)),
            _const_spec((CONV_WIDTH, 2 * dff)),
            _const_spec((1, 2 * dff)),
            _const_spec((dff, d)),
            _const_spec((1, d)),
        ],
        out_specs=pl.BlockSpec((tm, d), row),
        scratch_shapes=[
            pltpu.VMEM((SUBLANES, 2 * dff), F32),
            pltpu.VMEM((tm + SUBLANES, cbw), F32),
            pltpu.VMEM((tm, dff), BF16),
        ],
        compiler_params=_cparams(("arbitrary",)),
        name="ffn",
    )(x1, h2, w_up, conv_w, conv_b, w_down, final_norm)


def _row_tile(s, want):
    t = min(want, s)
    while s % t:
        t //= 2
    return t


def _tiles(s):
    return dict(
        inproj_rows=_row_tile(s, 2048),
        inproj_cols=N_PROJ // 13,
        hgrn_rows=_row_tile(s, 512),
        prep_rows=_row_tile(s, 512),
        merge_rows=_row_tile(s, 512),
        ffn_rows=_row_tile(s, 512),
        ffn_cols=256,
    )


def kernel(x, attn_norm, w_in, lower_bounds, hgrn_out_norm, q_norm, kv_norm, w_uq, w_uk, w_uv, w_iq,
           rel_bias, w_branch_a, w_branch_b, w_o, ffn_norm, w_up, conv_w, conv_b, w_down, final_norm):
    bsz, s, d = x.shape
    assert bsz == 1 and attn_norm.shape[0] == 1 and lower_bounds.shape[0] == 2
    x2 = x.reshape(s, d)
    top_k = min(TOPK_MAX, s // 4)

    wi = w_in[0]
    n_small = Q_RANK + KV_RANK + IDX_DIM + IDX_HEADS
    w_pad = jnp.concatenate(
        [wi[:, :COL_GATE].astype(BF16), wi[:, COL_GATE + n_small:].astype(BF16),
         wi[:, COL_GATE:COL_GATE + n_small].astype(BF16), jnp.zeros((d, SMALL_W - n_small), BF16)], axis=1)

    t = _tiles(s)
    proj = _inproj(x2, attn_norm, w_pad, t["inproj_rows"], t["inproj_cols"])

    hg = _hgrn(proj, lower_bounds, hgrn_out_norm, t["hgrn_rows"], A_HEADS)

    qlat, qidxt, widxt, ckvx, ckvt, kidx = _mla_prep(
        proj, q_norm, kv_norm, w_uq[0].astype(BF16), jnp.swapaxes(w_uk[0], 1, 2).astype(BF16),
        w_iq[0].astype(BF16), t["prep_rows"])

    t_loc = jnp.arange(TQ, dtype=jnp.int32)[None, :, None]
    s_loc = jnp.arange(TKN, dtype=jnp.int32)[None, None, :]
    off = jnp.array([0, -TQ], jnp.int32)[:, None, None]
    buckets = _t5_bucket(s_loc - t_loc + off)
    biasd = _bias_tab(buckets, rel_bias)

    ob = _dsa(qlat, qidxt, widxt, kidx, ckvt, ckvx, biasd, w_uv[0].astype(BF16), top_k)

    x1, h2 = _merge(x2, hg, ob, proj, w_branch_a[0].astype(BF16), w_branch_b[0].astype(BF16),
                    w_o[0].astype(BF16), ffn_norm, t["merge_rows"])

    out = _ffn(x1, h2, w_up[0].astype(BF16), conv_w[0], conv_b, w_down[0].astype(BF16),
               final_norm.reshape(1, d), t["ffn_rows"], t["ffn_cols"])
    return out.reshape(bsz, s, d)
```
